```python
import math
import jax, jax.numpy as jnp
from jax import lax
import numpy as np

D_MODEL = 2048
BATCH = 2
SEQ = 8192
DEPTH = 2

N_MIXERS = 2
N_DIFF_LAYERS = (DEPTH + 1) // 2
N_NSA_LAYERS = DEPTH // 2

HEAD_DIM = 128
ROPE_THETA = 10000.0
Q_BLOCK = 128

DIFF_HEADS = D_MODEL // (2 * HEAD_DIM)
DIFF_V_DIM = 2 * HEAD_DIM
DIFF_IN = 4 * DIFF_HEADS * HEAD_DIM + DIFF_HEADS * DIFF_V_DIM

NSA_HEADS = D_MODEL // HEAD_DIM
NSA_GROUPS = 4
NSA_HPG = NSA_HEADS // NSA_GROUPS
NSA_KV = NSA_GROUPS * HEAD_DIM
N_BRANCH = 3
CMP_LEN = 32
CMP_STRIDE = 16
SEL_LEN = 64
N_SELECT = 16
WINDOW = 512
NSA_IN = NSA_HEADS * HEAD_DIM + N_BRANCH * 2 * NSA_KV + NSA_HEADS * N_BRANCH

N_EXPERTS = 32
TOP_K = 4
D_FF = D_MODEL
SWIGLU_LIMIT = 7.0
SWIGLU_ALPHA = 1.702
MOE_BLOCK = 128

DEEPNORM_ALPHA = (2 * DEPTH) ** 0.25
DEEPNORM_BETA = (8 * DEPTH) ** -0.25
LN_EPS = 1e-5
RMS_EPS = 1e-5
NEG_INF = -1e30

kernel_name = 'hybrid_diffattn_nsa_moe_deepnorm'


def layer_norm(x, g, b):
    xf = x.astype(jnp.float32)
    mu = jnp.mean(xf, -1, keepdims=True)
    xc = xf - mu
    var = jnp.mean(xc * xc, -1, keepdims=True)
    return (xc * lax.rsqrt(var + LN_EPS) * g + b).astype(x.dtype)


def rope_tables(positions):
    inv = ROPE_THETA ** (-jnp.arange(0, HEAD_DIM, 2, dtype=jnp.float32) / HEAD_DIM)
    ang = positions.astype(jnp.float32)[..., None] * inv
    return jnp.cos(ang)[:, :, None, :], jnp.sin(ang)[:, :, None, :]


def apply_rope(t, cos, sin):
    half = t.shape[-1] // 2
    t1 = t[..., :half].astype(jnp.float32)
    t2 = t[..., half:].astype(jnp.float32)
    return jnp.concatenate([t1 * cos - t2 * sin, t2 * cos + t1 * sin], -1).astype(t.dtype)


def masked_softmax(s, mask):
    p = jax.nn.softmax(jnp.where(mask, s, NEG_INF), axis=-1)
    return jnp.where(mask, p, 0.0)


def diff_attention(x, cos, sin, w_in, lq1, lk1, lq2, lk2, subln_g, w_o, lambda_init):
    B, S, _ = x.shape
    H, dh = DIFF_HEADS, HEAD_DIM
    nq = S // Q_BLOCK
    proj = x @ w_in
    w = H * dh
    q1, q2, k1, k2 = [apply_rope(proj[..., i * w:(i + 1) * w].reshape(B, S, H, dh), cos, sin)
                      for i in range(4)]
    v = proj[..., 4 * w:].reshape(B, S, H, DIFF_V_DIM)
    q = jnp.stack([q1, q2], axis=2)
    k = jnp.stack([k1, k2], axis=2)
    lam = (jnp.exp(jnp.sum(lq1.astype(jnp.float32) * lk1.astype(jnp.float32)))
           - jnp.exp(jnp.sum(lq2.astype(jnp.float32) * lk2.astype(jnp.float32))) + lambda_init)
    scale = dh ** -0.5
    qb_all = q.reshape(B, nq, Q_BLOCK, 2, H, dh).transpose(1, 0, 2, 3, 4, 5)
    kpos = jnp.arange(S)

    def block(args):
        qb, i = args
        t = i * Q_BLOCK + jnp.arange(Q_BLOCK)
        s = jnp.einsum('bqmhd,bkmhd->bmhqk', qb, k).astype(jnp.float32) * scale
        mask = kpos[None, :] <= t[:, None]
        p = jax.nn.softmax(jnp.where(mask, s, NEG_INF), axis=-1)
        a = p[:, 0] - lam * p[:, 1]
        return jnp.einsum('bhqk,bkhv->bqhv', a.astype(v.dtype), v)

    out = lax.map(block, (qb_all, jnp.arange(nq)))
    out = out.transpose(1, 0, 2, 3, 4).reshape(B, S, H, DIFF_V_DIM).astype(jnp.float32)
    out = out * lax.rsqrt(jnp.mean(out * out, -1, keepdims=True) + RMS_EPS) * subln_g
    out = (out * (1.0 - lambda_init)).astype(x.dtype)
    return out.reshape(B, S, H * DIFF_V_DIM) @ w_o


def selection_scores(p_cmp, n_sel):
    R = SEL_LEN // CMP_STRIDE
    C = CMP_LEN // CMP_STRIDE
    n_cmp = p_cmp.shape[-1]
    right = R * n_sel + R + C - n_cmp
    pp = jnp.pad(p_cmp, [(0, 0)] * (p_cmp.ndim - 1) + [(C - 1, right)])
    out = jnp.zeros(p_cmp.shape[:-1] + (n_sel,), p_cmp.dtype)
    for m in range(R):
        for n in range(C):
            start = C - 1 + m - n
            out = out + pp[..., start:start + R * n_sel:R]
    return out


def nsa_attention(x, cos, sin, w_in, ck_pos, ck_w1, ck_w2, cv_pos, cv_w1, cv_w2, w_o):
    B, S, _ = x.shape
    H, G, hpg, dh = NSA_HEADS, NSA_GROUPS, NSA_HPG, HEAD_DIM
    nq = S // Q_BLOCK
    n_cmp = (S - CMP_LEN) // CMP_STRIDE + 1
    n_sel = S // SEL_LEN
    n_top = min(N_SELECT, n_sel)
    scale = dh ** -0.5
    proj = x @ w_in
    off = H * dh
    q = apply_rope(proj[..., :off].reshape(B, S, H, dh), cos, sin)
    kv = proj[..., off:off + 6 * NSA_KV].reshape(B, S, N_BRANCH, 2, G, dh)
    off = off + 6 * NSA_KV
    gates = jax.nn.sigmoid(proj[..., off:].astype(jnp.float32)).reshape(B, S, H, N_BRANCH)
    k_cmp, k_sel, k_win = [apply_rope(kv[:, :, c, 0], cos, sin) for c in range(N_BRANCH)]
    v_cmp, v_sel, v_win = [kv[:, :, c, 1] for c in range(N_BRANCH)]

    cidx = jnp.arange(n_cmp)[:, None] * CMP_STRIDE + jnp.arange(CMP_LEN)[None, :]

    def compress(t, pos, w1, w2):
        blocks = t[:, cidx] + pos[None, None, :, None, :]
        flat = blocks.transpose(0, 1, 3, 2, 4).reshape(B, n_cmp, G, CMP_LEN * dh)
        return (jax.nn.silu(flat @ w1) @ w2).transpose(0, 2, 1, 3)

    kc = compress(k_cmp, ck_pos, ck_w1, ck_w2)
    vc = compress(v_cmp, cv_pos, cv_w1, cv_w2)
    ks_b = k_sel.transpose(0, 2, 1, 3).reshape(B, G, n_sel, SEL_LEN, dh)
    vs_b = v_sel.transpose(0, 2, 1, 3).reshape(B, G, n_sel, SEL_LEN, dh)
    kw_pad = jnp.pad(k_win.transpose(0, 2, 1, 3), ((0, 0), (0, 0), (WINDOW, 0), (0, 0)))
    vw_pad = jnp.pad(v_win.transpose(0, 2, 1, 3), ((0, 0), (0, 0), (WINDOW, 0), (0, 0)))
    qg = q.reshape(B, S, G, hpg, dh).transpose(0, 2, 3, 1, 4)
    cend = jnp.arange(n_cmp) * CMP_STRIDE + CMP_LEN - 1
    blk = jnp.arange(n_sel)
    bi = jnp.arange(B)[:, None, None, None]
    gi = jnp.arange(G)[None, :, None, None]

    def block(i):
        s0 = i * Q_BLOCK
        t = s0 + jnp.arange(Q_BLOCK)
        qb = lax.dynamic_slice_in_dim(qg, s0, Q_BLOCK, axis=3)
        sc = jnp.einsum('bghqd,bgnd->bghqn', qb, kc).astype(jnp.float32) * scale
        pc = masked_softmax(sc, cend[None, :] <= t[:, None])
        o_cmp = jnp.einsum('bghqn,bgnd->bghqd', pc.astype(vc.dtype), vc)
        imp = selection_scores(jnp.sum(pc, axis=2), n_sel)
        cur = (t // SEL_LEN)[:, None]
        valid = blk[None, :] * SEL_LEN <= t[:, None]
        forced = (blk[None, :] == 0) | (blk[None, :] == cur) | (blk[None, :] == cur - 1)
        score = jnp.where(valid, jnp.where(forced, jnp.inf, imp), -1.0)
        _, sel = lax.top_k(score, n_top)
        ks = ks_b[bi, gi, sel]
        vs = vs_b[bi, gi, sel]
        spos = sel[..., None] * SEL_LEN + jnp.arange(SEL_LEN)
        smask = (spos <= t[None, None, :, None, None]).reshape(B, G, Q_BLOCK, n_top * SEL_LEN)[:, :, None]
        ss = jnp.einsum('bghqd,bgqnld->bghqnl', qb, ks).astype(jnp.float32) * scale
        ps = masked_softmax(ss.reshape(B, G, hpg, Q_BLOCK, n_top * SEL_LEN), smask)
        o_sel = jnp.einsum('bghqnl,bgqnld->bghqd',
                           ps.reshape(B, G, hpg, Q_BLOCK, n_top, SEL_LEN).astype(vs.dtype), vs)
        kwb = lax.dynamic_slice_in_dim(kw_pad, s0, Q_BLOCK + WINDOW, axis=2)
        vwb = lax.dynamic_slice_in_dim(vw_pad, s0, Q_BLOCK + WINDOW, axis=2)
        wpos = s0 - WINDOW + jnp.arange(Q_BLOCK + WINDOW)
        wmask = ((wpos[None, :] <= t[:, None]) & (wpos[None, :] > t[:, None] - WINDOW)
                 & (wpos[None, :] >= 0))
        sw = jnp.einsum('bghqd,bgkd->bghqk', qb, kwb).astype(jnp.float32) * scale
        pw = masked_softmax(sw, wmask)
        o_win = jnp.einsum('bghqk,bgkd->bghqd', pw.astype(vwb.dtype), vwb)
        return jnp.stack([o_cmp, o_sel, o_win], axis=-2)

    outs = lax.map(block, jnp.arange(nq))
    outs = outs.transpose(1, 0, 4, 2, 3, 5, 6).reshape(B, S, H, N_BRANCH, dh)
    y = jnp.sum(outs.astype(jnp.float32) * gates[..., None], axis=3).astype(x.dtype)
    return y.reshape(B, S, H * dh) @ w_o


def moe_ffn(x, li, router_w, router_b, w_gu, b_gu, w_down, b_down):
    B, S, D = x.shape
    T = B * S
    xf = x.reshape(T, D)
    logits = (xf @ router_w[li] + router_b[li]).astype(jnp.float32)
    top_val, top_idx = lax.top_k(logits, TOP_K)
    gate = jax.nn.softmax(top_val, axis=-1)
    n_rows = T * TOP_K
    n_blocks = -(-(n_rows + N_EXPERTS * (MOE_BLOCK - 1)) // MOE_BLOCK)
    n_pad = n_blocks * MOE_BLOCK
    flat_e = top_idx.reshape(-1).astype(jnp.int32)
    flat_t = jnp.repeat(jnp.arange(T, dtype=jnp.int32), TOP_K)
    flat_w = gate.reshape(-1)
    counts = jnp.zeros((N_EXPERTS,), jnp.int32).at[flat_e].add(1)
    padded = ((counts + MOE_BLOCK - 1) // MOE_BLOCK) * MOE_BLOCK
    pad_end = jnp.cumsum(padded)
    pad_start = pad_end - padded
    grp_start = jnp.cumsum(counts) - counts
    order = jnp.argsort(flat_e, stable=True)
    se = flat_e[order]
    dest = pad_start[se] + (jnp.arange(n_rows, dtype=jnp.int32) - grp_start[se])
    row_tok = jnp.full((n_pad,), T, jnp.int32).at[dest].set(flat_t[order])
    row_w = jnp.zeros((n_pad,), jnp.float32).at[dest].set(flat_w[order])
    block_e = jnp.minimum(jnp.searchsorted(pad_end, jnp.arange(n_blocks) * MOE_BLOCK, side='right'),
                          N_EXPERTS - 1).astype(jnp.int32)
    x_pad = jnp.concatenate([xf, jnp.zeros((1, D), xf.dtype)], axis=0)
    x_rows = x_pad[row_tok].reshape(n_blocks, MOE_BLOCK, D)

    def expert_block(args):
        xb, e = args
        hgu = xb @ w_gu[li, e] + b_gu[li, e]
        g = jnp.minimum(hgu[:, :D_FF], SWIGLU_LIMIT)
        u = jnp.clip(hgu[:, D_FF:], -SWIGLU_LIMIT, SWIGLU_LIMIT)
        act = g * jax.nn.sigmoid(SWIGLU_ALPHA * g) * (u + 1.0)
        return act @ w_down[li, e] + b_down[li, e]

    y_rows = lax.map(expert_block, (x_rows, block_e)).reshape(n_pad, D)
    y = jnp.zeros((T + 1, D), x.dtype).at[row_tok].add(y_rows * row_w[:, None].astype(x.dtype))
    return y[:T].reshape(B, S, D)


def setup_inputs(seed: int = 0) -> dict:
    key = jax.random.key(seed)
    ks = jax.random.split(key, 24)
    f32 = jnp.float32
    D = D_MODEL

    def nrm(k, shape, scale):
        return jax.random.normal(k, shape, f32) * scale

    x = nrm(ks[0], (BATCH, SEQ, D), 1.0)
    positions = (jax.random.randint(ks[1], (BATCH, 1), 0, 1024, dtype=jnp.int32)
                 + jnp.arange(SEQ, dtype=jnp.int32)[None, :])
    ln_gain = 1.0 + nrm(ks[2], (DEPTH, 2, D), 0.02)
    ln_bias = nrm(ks[3], (DEPTH, 2, D), 0.02)
    diff_w_in = nrm(ks[4], (N_DIFF_LAYERS, D, DIFF_IN), D ** -0.5)
    lam_keys = jax.random.split(ks[5], 4)
    diff_lambda_q1 = nrm(lam_keys[0], (N_DIFF_LAYERS, HEAD_DIM), 0.1)
    diff_lambda_k1 = nrm(lam_keys[1], (N_DIFF_LAYERS, HEAD_DIM), 0.1)
    diff_lambda_q2 = nrm(lam_keys[2], (N_DIFF_LAYERS, HEAD_DIM), 0.1)
    diff_lambda_k2 = nrm(lam_keys[3], (N_DIFF_LAYERS, HEAD_DIM), 0.1)
    diff_subln_gain = 1.0 + nrm(ks[6], (N_DIFF_LAYERS, DIFF_V_DIM), 0.02)
    diff_w_o = nrm(ks[7], (N_DIFF_LAYERS, DIFF_HEADS * DIFF_V_DIM, D),
                   (DIFF_HEADS * DIFF_V_DIM) ** -0.5 * DEEPNORM_BETA)
    nsa_w_in = nrm(ks[8], (N_NSA_LAYERS, D, NSA_IN), D ** -0.5)
    nsa_cmp_k_pos = nrm(ks[9], (N_NSA_LAYERS, CMP_LEN, HEAD_DIM), 0.1)
    nsa_cmp_k_w1 = nrm(ks[10], (N_NSA_LAYERS, CMP_LEN * HEAD_DIM, HEAD_DIM), (CMP_LEN * HEAD_DIM) ** -0.5)
    nsa_cmp_k_w2 = nrm(ks[11], (N_NSA_LAYERS, HEAD_DIM, HEAD_DIM), HEAD_DIM ** -0.5)
    nsa_cmp_v_pos = nrm(ks[12], (N_NSA_LAYERS, CMP_LEN, HEAD_DIM), 0.1)
    nsa_cmp_v_w1 = nrm(ks[13], (N_NSA_LAYERS, CMP_LEN * HEAD_DIM, HEAD_DIM), (CMP_LEN * HEAD_DIM) ** -0.5)
    nsa_cmp_v_w2 = nrm(ks[14], (N_NSA_LAYERS, HEAD_DIM, HEAD_DIM), HEAD_DIM ** -0.5)
    nsa_w_o = nrm(ks[15], (N_NSA_LAYERS, NSA_HEADS * HEAD_DIM, D),
                  (NSA_HEADS * HEAD_DIM) ** -0.5 * DEEPNORM_BETA)
    moe_router_w = nrm(ks[16], (DEPTH, D, N_EXPERTS), D ** -0.5)
    moe_router_b = nrm(ks[17], (DEPTH, N_EXPERTS), 0.01)
    moe_w_gate_up = nrm(ks[18], (DEPTH, N_EXPERTS, D, 2 * D_FF), D ** -0.5)
    moe_b_gate_up = nrm(ks[19], (DEPTH, N_EXPERTS, 2 * D_FF), 0.01)
    moe_w_down = nrm(ks[20], (DEPTH, N_EXPERTS, D_FF, D), D_FF ** -0.5 * DEEPNORM_BETA)
    moe_b_down = nrm(ks[21], (DEPTH, N_EXPERTS, D), 0.01)
    return {'x': x, 'positions': positions, 'ln_gain': ln_gain, 'ln_bias': ln_bias,
            'diff_w_in': diff_w_in, 'diff_lambda_q1': diff_lambda_q1, 'diff_lambda_k1': diff_lambda_k1,
            'diff_lambda_q2': diff_lambda_q2, 'diff_lambda_k2': diff_lambda_k2,
            'diff_subln_gain': diff_subln_gain, 'diff_w_o': diff_w_o,
            'nsa_w_in': nsa_w_in, 'nsa_cmp_k_pos': nsa_cmp_k_pos, 'nsa_cmp_k_w1': nsa_cmp_k_w1,
            'nsa_cmp_k_w2': nsa_cmp_k_w2, 'nsa_cmp_v_pos': nsa_cmp_v_pos, 'nsa_cmp_v_w1': nsa_cmp_v_w1,
            'nsa_cmp_v_w2': nsa_cmp_v_w2, 'nsa_w_o': nsa_w_o,
            'moe_router_w': moe_router_w, 'moe_router_b': moe_router_b,
            'moe_w_gate_up': moe_w_gate_up, 'moe_b_gate_up': moe_b_gate_up,
            'moe_w_down': moe_w_down, 'moe_b_down': moe_b_down}


def reference(x, positions, ln_gain, ln_bias,
              diff_w_in, diff_lambda_q1, diff_lambda_k1, diff_lambda_q2, diff_lambda_k2,
              diff_subln_gain, diff_w_o,
              nsa_w_in, nsa_cmp_k_pos, nsa_cmp_k_w1, nsa_cmp_k_w2,
              nsa_cmp_v_pos, nsa_cmp_v_w1, nsa_cmp_v_w2, nsa_w_o,
              moe_router_w, moe_router_b, moe_w_gate_up, moe_b_gate_up, moe_w_down, moe_b_down):
    cos, sin = rope_tables(positions)
    h = x
    for li in range(DEPTH):
        j = li // N_MIXERS
        if li % N_MIXERS == 0:
            lambda_init = 0.8 - 0.6 * math.exp(-0.3 * li)
            mix = diff_attention(h, cos, sin, diff_w_in[j], diff_lambda_q1[j], diff_lambda_k1[j],
                                 diff_lambda_q2[j], diff_lambda_k2[j], diff_subln_gain[j],
                                 diff_w_o[j], lambda_init)
        else:
            mix = nsa_attention(h, cos, sin, nsa_w_in[j], nsa_cmp_k_pos[j], nsa_cmp_k_w1[j],
                                nsa_cmp_k_w2[j], nsa_cmp_v_pos[j], nsa_cmp_v_w1[j],
                                nsa_cmp_v_w2[j], nsa_w_o[j])
        h = layer_norm(DEEPNORM_ALPHA * h + mix, ln_gain[li, 0], ln_bias[li, 0])
        ffn = moe_ffn(h, li, moe_router_w, moe_router_b, moe_w_gate_up, moe_b_gate_up,
                      moe_w_down, moe_b_down)
        h = layer_norm(DEEPNORM_ALPHA * h + ffn, ln_gain[li, 1], ln_bias[li, 1])
    return h
```

```python
import functools
import math

import jax
import jax.numpy as jnp
from jax import lax
from jax.experimental import pallas as pl
from jax.experimental.pallas import tpu as pltpu

F32 = jnp.float32
BF16 = jnp.bfloat16
I32 = jnp.int32

D_MODEL = 2048
DEPTH = 2
HEAD_DIM = 128
ROPE_THETA = 10000.0

DIFF_HEADS = D_MODEL // (2 * HEAD_DIM)
DIFF_V_DIM = 2 * HEAD_DIM

NSA_HEADS = D_MODEL // HEAD_DIM
NSA_GROUPS = 4
NSA_HPG = NSA_HEADS // NSA_GROUPS
NSA_KV = NSA_GROUPS * HEAD_DIM
N_BRANCH = 3
CMP_LEN = 32
CMP_STRIDE = 16
SEL_LEN = 64
N_SELECT = 16
WINDOW = 512
NSA_Q_BLOCK = 128

N_EXPERTS = 32
TOP_K = 4
D_FF = D_MODEL
SWIGLU_LIMIT = 7.0
SWIGLU_ALPHA = 1.702

DEEPNORM_ALPHA = (2 * DEPTH) ** 0.25
LN_EPS = 1e-5
RMS_EPS = 1e-5
NEG = -1e30
LOG2E = 1.4426950408889634
SM_SCALE_LOG2 = (HEAD_DIM ** -0.5) * LOG2E

LANES = 128
VMEM_LIMIT = 56 * 1024 * 1024
MOE_ROWS = 256


def _cparams(sem):
    return pltpu.CompilerParams(dimension_semantics=sem, vmem_limit_bytes=VMEM_LIMIT)


def _proj_kernel(n_rope_tiles, a_ref, w_ref, cos_ref, sin_ref, o_ref):
    j = pl.program_id(0)
    y = jnp.dot(a_ref[...], w_ref[...], preferred_element_type=F32)
    tn = y.shape[1]

    @pl.when(j < n_rope_tiles)
    def _():
        cos = cos_ref[...]
        sin = sin_ref[...]
        for c in range(tn // HEAD_DIM):
            t = y[:, c * HEAD_DIM:(c + 1) * HEAD_DIM]
            r = t * cos + pltpu.roll(t, HEAD_DIM // 2, 1) * sin
            o_ref[:, c * HEAD_DIM:(c + 1) * HEAD_DIM] = r.astype(o_ref.dtype)

    @pl.when(j >= n_rope_tiles)
    def _():
        o_ref[...] = y.astype(o_ref.dtype)


def _proj(a, w, cos2, sin2, n_rope_tiles, out_dtype, tm=1024, tn=512):
    M, K = a.shape
    N = w.shape[1]
    tm = min(tm, M)
    return pl.pallas_call(
        functools.partial(_proj_kernel, n_rope_tiles),
        out_shape=jax.ShapeDtypeStruct((M, N), out_dtype),
        grid=(N // tn, M // tm),
        in_specs=[
            pl.BlockSpec((tm, K), lambda j, i: (i, 0)),
            pl.BlockSpec((K, tn), lambda j, i: (0, j)),
            pl.BlockSpec((tm, HEAD_DIM), lambda j, i: (i, 0)),
            pl.BlockSpec((tm, HEAD_DIM), lambda j, i: (i, 0)),
        ],
        out_specs=pl.BlockSpec((tm, tn), lambda j, i: (i, j)),
        compiler_params=_cparams(("arbitrary", "arbitrary")),
        name="proj_rope",
    )(a, w, cos2, sin2)


def _gate_kernel(a_ref, w_ref, o_ref):
    y = jnp.dot(a_ref[...], w_ref[...], preferred_element_type=F32)
    o_ref[...] = jax.nn.sigmoid(y)


def _gate_proj(a, w, tm=1024):
    M, K = a.shape
    N = w.shape[1]
    tm = min(tm, M)
    return pl.pallas_call(
        _gate_kernel,
        out_shape=jax.ShapeDtypeStruct((M, N), F32),
        grid=(M // tm,),
        in_specs=[pl.BlockSpec((tm, K), lambda i: (i, 0)),
                  pl.BlockSpec((K, N), lambda i: (0, 0))],
        out_specs=pl.BlockSpec((tm, N), lambda i: (i, 0)),
        compiler_params=_cparams(("arbitrary",)),
        name="nsa_gates",
    )(a, w)


def _diff_attn_kernel(lambda_init, tq, q1_ref, q2_ref, k1_ref, k2_ref, v_ref,
                      lam_ref, g_ref, o_ref, acc1, acc2):
    qi = pl.program_id(2)
    tk = tq
    q1 = q1_ref[...]
    q2 = q2_ref[...]
    acc1[...] = jnp.zeros_like(acc1)
    acc2[...] = jnp.zeros_like(acc2)

    def tile(kt, carry, masked):
        off = pl.multiple_of(kt * tk, tk)
        v = v_ref[pl.ds(off, tk), :]

        def one(q, k_ref, m, l, acc):
            k = k_ref[pl.ds(off, tk), :]
            s = lax.dot_general(q, k, (((1,), (1,)), ((), ())), preferred_element_type=F32)
            if masked:
                row = lax.broadcasted_iota(I32, s.shape, 0)
                col = lax.broadcasted_iota(I32, s.shape, 1)
                s = jnp.where(col <= row, s, NEG)
            m_new = jnp.maximum(m, jnp.max(s, axis=1, keepdims=True))
            alpha = jnp.exp2((m - m_new) * SM_SCALE_LOG2)
            p = jnp.exp2((s - m_new) * SM_SCALE_LOG2)
            l_new = alpha * l + jnp.sum(p, axis=1, keepdims=True)
            acc[...] = alpha * acc[...] + jnp.dot(p.astype(BF16), v, preferred_element_type=F32)
            return m_new, l_new

        m1, l1, m2, l2 = carry
        m1, l1 = one(q1, k1_ref, m1, l1, acc1)
        m2, l2 = one(q2, k2_ref, m2, l2, acc2)
        return m1, l1, m2, l2

    neg = jnp.full((tq, 1), NEG, F32)
    zero = jnp.zeros((tq, 1), F32)
    carry = lax.fori_loop(0, qi, lambda kt, c: tile(kt, c, False), (neg, zero, neg, zero))
    _, l1, _, l2 = tile(qi, carry, True)

    lamv = lam_ref[...]
    lam = (jnp.exp(jnp.sum(lamv[0:1] * lamv[1:2], axis=1, keepdims=True))
           - jnp.exp(jnp.sum(lamv[2:3] * lamv[3:4], axis=1, keepdims=True)) + lambda_init)
    a = acc1[...] / l1 - lam * (acc2[...] / l2)
    ms = jnp.mean(a * a, axis=1, keepdims=True)
    o = a * lax.rsqrt(ms + RMS_EPS) * g_ref[...]
    o_ref[...] = (o * (1.0 - lambda_init)).astype(o_ref.dtype)


def _diff_attention(proj, lamv, subln_g, lambda_init, B, S, tq=512):
    H = DIFF_HEADS
    tq = min(tq, S)
    nq = S // tq
    T = B * S
    kern = functools.partial(_diff_attn_kernel, lambda_init, tq)
    return pl.pallas_call(
        kern,
        out_shape=jax.ShapeDtypeStruct((T, H * DIFF_V_DIM), BF16),
        grid=(B, H, nq),
        in_specs=[
            pl.BlockSpec((tq, HEAD_DIM), lambda b, h, q: (b * nq + q, h)),
            pl.BlockSpec((tq, HEAD_DIM), lambda b, h, q: (b * nq + q, H + h)),
            pl.BlockSpec((S, HEAD_DIM), lambda b, h, q: (b, 2 * H + h)),
            pl.BlockSpec((S, HEAD_DIM), lambda b, h, q: (b, 3 * H + h)),
            pl.BlockSpec((S, DIFF_V_DIM), lambda b, h, q: (b, 2 * H + h)),
            pl.BlockSpec((4, HEAD_DIM), lambda b, h, q: (0, 0)),
            pl.BlockSpec((1, DIFF_V_DIM), lambda b, h, q: (0, 0)),
        ],
        out_specs=pl.BlockSpec((tq, DIFF_V_DIM), lambda b, h, q: (b * nq + q, h)),
        scratch_shapes=[pltpu.VMEM((tq, DIFF_V_DIM), F32), pltpu.VMEM((tq, DIFF_V_DIM), F32)],
        compiler_params=_cparams(("arbitrary", "arbitrary", "arbitrary")),
        name="diff_attn",
    )(proj, proj, proj, proj, proj, lamv, subln_g)


def _layer_norm(z, g, b):
    mu = jnp.mean(z, axis=-1, keepdims=True)
    zc = z - mu
    var = jnp.mean(zc * zc, axis=-1, keepdims=True)
    return zc * lax.rsqrt(var + LN_EPS) * g + b


def _route(hn, rw_hi_ref, rw_lo_ref, rb_ref, topi_ref, gate_ref):
    hi = hn.astype(BF16)
    lo = (hn - hi.astype(F32)).astype(BF16)
    w_hi = rw_hi_ref[...]
    logits = (jnp.dot(hi, w_hi, preferred_element_type=F32)
              + jnp.dot(lo, w_hi, preferred_element_type=F32)
              + jnp.dot(hi, rw_lo_ref[...], preferred_element_type=F32)) + rb_ref[...]
    lane = lax.broadcasted_iota(I32, logits.shape, 1).astype(F32)
    cur = jnp.where(lane < N_EXPERTS, logits, -jnp.inf)
    vals, idxs = [], []
    for _ in range(TOP_K):
        m = jnp.max(cur, axis=1, keepdims=True)
        idx = jnp.min(jnp.where(cur == m, lane, float(LANES)), axis=1, keepdims=True)
        vals.append(m)
        idxs.append(idx)
        cur = jnp.where(lane == idx, -jnp.inf, cur)
    es = [jnp.exp(v - vals[0]) for v in vals]
    den = es[0] + es[1] + es[2] + es[3]
    gate_out = jnp.zeros_like(logits)
    idx_out = jnp.zeros_like(logits)
    for k in range(TOP_K):
        gate_out = jnp.where(lane == float(k), es[k] / den, gate_out)
        idx_out = jnp.where(lane == float(k), idxs[k], idx_out)
    gate_ref[...] = gate_out
    topi_ref[...] = idx_out.astype(I32)


def _out_ln_kernel(a_ref, w_ref, h_ref, g_ref, b_ref, rw_hi_ref, rw_lo_ref, rb_ref,
                   o_ref, topi_ref, gate_ref):
    mix = jnp.dot(a_ref[...], w_ref[...], preferred_element_type=F32)
    hn = _layer_norm(DEEPNORM_ALPHA * h_ref[...] + mix, g_ref[...], b_ref[...])
    o_ref[...] = hn
    _route(hn, rw_hi_ref, rw_lo_ref, rb_ref, topi_ref, gate_ref)


def _out_ln_route(a, w_o, h, gain, bias, rw_hi, rw_lo, rb, tm=256):
    T, K = a.shape
    D = w_o.shape[1]
    tm = min(tm, T)
    row = lambda i: (i, 0)
    fixed = lambda i: (0, 0)
    return pl.pallas_call(
        _out_ln_kernel,
        out_shape=(jax.ShapeDtypeStruct((T, D), F32),
                   jax.ShapeDtypeStruct((T, LANES), I32),
                   jax.ShapeDtypeStruct((T, LANES), F32)),
        grid=(T // tm,),
        in_specs=[
            pl.BlockSpec((tm, K), row),
            pl.BlockSpec((K, D), fixed),
            pl.BlockSpec((tm, D), row),
            pl.BlockSpec((1, D), fixed),
            pl.BlockSpec((1, D), fixed),
            pl.BlockSpec((D, LANES), fixed),
            pl.BlockSpec((D, LANES), fixed),
            pl.BlockSpec((1, LANES), fixed),
        ],
        out_specs=(pl.BlockSpec((tm, D), row), pl.BlockSpec((tm, LANES), row),
                   pl.BlockSpec((tm, LANES), row)),
        compiler_params=_cparams(("arbitrary",)),
        name="out_ln_route",
    )(a, w_o, h, gain, bias, rw_hi, rw_lo, rb)


def _dispatch_kernel(tmd, dest_ref, h_ref, xin_ref, xout_ref, sem):
    del xin_ref

    def row_copy(t, d):
        return pltpu.make_async_copy(h_ref.at[pl.ds(t, 1), :], xout_ref.at[pl.ds(d, 1), :], sem)

    def issue(t, c):
        for k in range(TOP_K):
            row_copy(t, dest_ref[t * TOP_K + k]).start()
        return c

    def drain(t, c):
        for k in range(TOP_K):
            row_copy(t, dest_ref[t * TOP_K + k]).wait()
        return c

    lax.fori_loop(0, tmd, issue, 0)
    lax.fori_loop(0, tmd, drain, 0)


def _dispatch(h, dest_flat, n_pad, tmd=256):
    T, D = h.shape
    tmd = min(tmd, T)
    zeros = jnp.zeros((n_pad, D), h.dtype)
    return pl.pallas_call(
        functools.partial(_dispatch_kernel, tmd),
        out_shape=jax.ShapeDtypeStruct((n_pad, D), h.dtype),
        grid=(T // tmd,),
        in_specs=[
            pl.BlockSpec((tmd * TOP_K,), lambda i: (i,), memory_space=pltpu.SMEM),
            pl.BlockSpec((tmd, D), lambda i: (i, 0)),
            pl.BlockSpec(memory_space=pl.ANY),
        ],
        out_specs=pl.BlockSpec(memory_space=pl.ANY),
        scratch_shapes=[pltpu.SemaphoreType.DMA(())],
        input_output_aliases={2: 0},
        compiler_params=_cparams(("arbitrary",)),
        name="moe_dispatch",
    )(dest_flat, h, zeros)


def _moe_up_kernel(be_ref, nu_ref, x_ref, wg_ref, wu_ref, bg_ref, bu_ref, o_ref):
    i = pl.program_id(1)

    @pl.when(i < nu_ref[0])
    def _():
        x = x_ref[...].astype(BF16)
        g = jnp.dot(x, wg_ref[...], preferred_element_type=F32) + bg_ref[...]
        u = jnp.dot(x, wu_ref[...], preferred_element_type=F32) + bu_ref[...]
        g = jnp.minimum(g, SWIGLU_LIMIT)
        u = jnp.clip(u, -SWIGLU_LIMIT, SWIGLU_LIMIT)
        act = g * jax.nn.sigmoid(SWIGLU_ALPHA * g) * (u + 1.0)
        o_ref[...] = act.astype(o_ref.dtype)

    @pl.when(i >= nu_ref[0])
    def _():
        o_ref[...] = jnp.zeros_like(o_ref)


def _moe_up(x_rows, w_gu, b_gu, block_e, n_used, li, tf=1024):
    n_pad, D = x_rows.shape
    R = MOE_ROWS
    nb = n_pad // R
    nf = D_FF // tf

    def xmap(j, i, be, nu):
        return (jnp.minimum(i, nu[0] - 1), 0)

    grid_spec = pltpu.PrefetchScalarGridSpec(
        num_scalar_prefetch=2,
        grid=(nf, nb),
        in_specs=[
            pl.BlockSpec((R, D), xmap),
            pl.BlockSpec((None, None, D, tf), lambda j, i, be, nu: (li, be[i], 0, j)),
            pl.BlockSpec((None, None, D, tf), lambda j, i, be, nu: (li, be[i], 0, nf + j)),
            pl.BlockSpec((None, None, 1, tf), lambda j, i, be, nu: (li, be[i], 0, j)),
            pl.BlockSpec((None, None, 1, tf), lambda j, i, be, nu: (li, be[i], 0, nf + j)),
        ],
        out_specs=pl.BlockSpec((R, tf), lambda j, i, be, nu: (i, j)),
    )
    return pl.pallas_call(
        _moe_up_kernel,
        out_shape=jax.ShapeDtypeStruct((n_pad, D_FF), BF16),
        grid_spec=grid_spec,
        compiler_params=_cparams(("arbitrary", "arbitrary")),
        name="moe_up",
    )(block_e, n_used, x_rows, w_gu, w_gu, b_gu, b_gu)


def _moe_down_kernel(be_ref, nu_ref, a_ref, w_ref, b_ref, o_ref):
    i = pl.program_id(1)

    @pl.when(i < nu_ref[0])
    def _():
        o_ref[...] = jnp.dot(a_ref[...], w_ref[...], preferred_element_type=F32) + b_ref[...]

    @pl.when(i >= nu_ref[0])
    def _():
        o_ref[...] = jnp.zeros_like(o_ref)


def _moe_down(act, w_down, b_down, block_e, n_used, li, tn=1024):
    n_pad, F = act.shape
    R = MOE_ROWS
    nb = n_pad // R
    D = D_MODEL

    def amap(j, i, be, nu):
        return (jnp.minimum(i, nu[0] - 1), 0)

    grid_spec = pltpu.PrefetchScalarGridSpec(
        num_scalar_prefetch=2,
        grid=(D // tn, nb),
        in_specs=[
            pl.BlockSpec((R, F), amap),
            pl.BlockSpec((None, None, F, tn), lambda j, i, be, nu: (li, be[i], 0, j)),
            pl.BlockSpec((None, None, 1, tn), lambda j, i, be, nu: (li, be[i], 0, j)),
        ],
        out_specs=pl.BlockSpec((R, tn), lambda j, i, be, nu: (i, j)),
    )
    return pl.pallas_call(
        _moe_down_kernel,
        out_shape=jax.ShapeDtypeStruct((n_pad, D), F32),
        grid_spec=grid_spec,
        compiler_params=_cparams(("arbitrary", "arbitrary")),
        name="moe_down",
    )(block_e, n_used, act, w_down, b_down)


def _combine_ln_kernel(tmc, dest_ref, y_ref, gate_ref, h_ref, g_ref, b_ref,
                       o_ref, obf_ref, buf, sem):
    def row_copy(t, k, d):
        return pltpu.make_async_copy(y_ref.at[pl.ds(d, 1), :], buf.at[k, pl.ds(t, 1), :], sem)

    def issue(t, c):
        for k in range(TOP_K):
            row_copy(t, k, dest_ref[t * TOP_K + k]).start()
        return c

    def drain(t, c):
        for k in range(TOP_K):
            row_copy(t, k, dest_ref[t * TOP_K + k]).wait()
        return c

    lax.fori_loop(0, tmc, issue, 0)
    lax.fori_loop(0, tmc, drain, 0)

    gate = gate_ref[...]
    ffn = gate[:, 0:1] * buf[0]
    for k in range(1, TOP_K):
        ffn = ffn + gate[:, k:k + 1] * buf[k]
    hn = _layer_norm(DEEPNORM_ALPHA * h_ref[...] + ffn, g_ref[...], b_ref[...])
    o_ref[...] = hn
    obf_ref[...] = hn.astype(BF16)


def _combine_ln(y_rows, dest_flat, gate, h, gain, bias, tmc=256):
    T, D = h.shape
    tmc = min(tmc, T)
    row = lambda i: (i, 0)
    fixed = lambda i: (0, 0)
    return pl.pallas_call(
        functools.partial(_combine_ln_kernel, tmc),
        out_shape=(jax.ShapeDtypeStruct((T, D), F32), jax.ShapeDtypeStruct((T, D), BF16)),
        grid=(T // tmc,),
        in_specs=[
            pl.BlockSpec((tmc * TOP_K,), lambda i: (i,), memory_space=pltpu.SMEM),
            pl.BlockSpec(memory_space=pl.ANY),
            pl.BlockSpec((tmc, LANES), row),
            pl.BlockSpec((tmc, D), row),
            pl.BlockSpec((1, D), fixed),
            pl.BlockSpec((1, D), fixed),
        ],
        out_specs=(pl.BlockSpec((tmc, D), row), pl.BlockSpec((tmc, D), row)),
        scratch_shapes=[pltpu.VMEM((TOP_K, tmc, D), F32), pltpu.SemaphoreType.DMA(())],
        compiler_params=_cparams(("arbitrary",)),
        name="moe_combine_ln",
    )(dest_flat, y_rows, gate, h, gain, bias)


def _moe_layer(h, topi, gate, li, w_gu, b_gu, w_down, b_down, gain, bias):
    T = h.shape[0]
    R = MOE_ROWS
    n_rows = T * TOP_K
    n_blocks = -(-(n_rows + N_EXPERTS * (R - 1)) // R)
    n_pad = n_blocks * R
    top_idx = topi[:, :TOP_K]
    onehot = (top_idx[:, :, None] == jnp.arange(N_EXPERTS, dtype=I32)[None, None, :]).astype(I32)
    oh = jnp.sum(onehot, axis=1)
    csum = jnp.cumsum(oh, axis=0)
    counts = csum[-1]
    padded = ((counts + R - 1) // R) * R
    pad_end = jnp.cumsum(padded)
    pad_start = pad_end - padded
    rank = jnp.take_along_axis(csum - oh, top_idx, axis=1)
    dest = (pad_start[top_idx] + rank).astype(I32).reshape(-1)
    block_e = jnp.minimum(
        jnp.searchsorted(pad_end, jnp.arange(n_blocks, dtype=I32) * R, side="right"),
        N_EXPERTS - 1).astype(I32)
    n_used = (pad_end[-1:] // R).astype(I32)

    x_rows = _dispatch(h, dest, n_pad)
    act = _moe_up(x_rows, w_gu, b_gu, block_e, n_used, li)
    y_rows = _moe_down(act, w_down, b_down, block_e, n_used, li)
    return _combine_ln(y_rows, dest, gate, h, gain, bias)


def _compress_one(c_ref, pos_ref, w1t_ref, w1b_ref, w2_ref, o_ref):
    c = c_ref[...]
    pos = pos_ref[...]
    a = jnp.dot((c + pos[0:1]).astype(BF16), w1t_ref[...], preferred_element_type=F32)
    b = jnp.dot((c + pos[1:2]).astype(BF16), w1b_ref[...], preferred_element_type=F32)
    n = a.shape[0]
    pre = a + pltpu.roll(b, n - 1, 0)
    hid = pre * jax.nn.sigmoid(pre)
    out = jnp.dot(hid.astype(BF16), w2_ref[...], preferred_element_type=F32)
    row = lax.broadcasted_iota(I32, out.shape, 0)
    o_ref[...] = jnp.where(row < n - 1, out, 0.0).astype(o_ref.dtype)


def _compress_kernel(ck_ref, cv_ref, pk_ref, pv_ref, k1t, k1b, k2, v1t, v1b, v2, ok_ref, ov_ref):
    _compress_one(ck_ref, pk_ref, k1t, k1b, k2, ok_ref)
    _compress_one(cv_ref, pv_ref, v1t, v1b, v2, ov_ref)


def _compress(ck, cv, pk, pv, k1t, k1b, k2, v1t, v1b, v2):
    B, G, NC, W = ck.shape
    blk = pl.BlockSpec((None, None, NC, W), lambda b, g: (b, g, 0, 0))
    oblk = pl.BlockSpec((None, None, NC, HEAD_DIM), lambda b, g: (b, g, 0, 0))
    full = lambda shape: pl.BlockSpec(shape, lambda b, g: (0, 0))
    return pl.pallas_call(
        _compress_kernel,
        out_shape=(jax.ShapeDtypeStruct((B, G, NC, HEAD_DIM), BF16),
                   jax.ShapeDtypeStruct((B, G, NC, HEAD_DIM), BF16)),
        grid=(B, G),
        in_specs=[blk, blk, full((2, W)), full((2, W)),
                  full((W, HEAD_DIM)), full((W, HEAD_DIM)), full((HEAD_DIM, HEAD_DIM)),
                  full((W, HEAD_DIM)), full((W, HEAD_DIM)), full((HEAD_DIM, HEAD_DIM))],
        out_specs=(oblk, oblk),
        compiler_params=_cparams(("arbitrary", "arbitrary")),
        name="nsa_compress",
    )(ck, cv, pk, pv, k1t, k1b, k2, v1t, v1b, v2)


def _nsa_attn_kernel(S, tk, q_ref, kc_ref, vc_ref, ks_ref, kw_ref, vs_ref, vw_ref,
                     gates_ref, imp_ref, o_ref, acc):
    qi = pl.program_id(2)
    QB = NSA_Q_BLOCK
    NC = S // CMP_STRIDE
    s0 = qi * QB
    nt = (((1,), (1,)), ((), ()))

    q = q_ref[...]
    qs = jnp.concatenate([q[:, h * HEAD_DIM:(h + 1) * HEAD_DIM] for h in range(NSA_HPG)], axis=0)
    rows = NSA_HPG * QB
    t_q = s0 + lax.broadcasted_iota(I32, (QB, 1), 0)

    def stack(m):
        return jnp.concatenate([m] * NSA_HPG, axis=0)

    sc = lax.dot_general(qs, kc_ref[...], nt, preferred_element_type=F32)
    ncol = lax.broadcasted_iota(I32, (QB, NC), 1)
    cmask = stack(ncol * CMP_STRIDE + (CMP_LEN - 1) <= t_q)
    scm = jnp.where(cmask, sc, NEG)
    mc = jnp.max(scm, axis=1, keepdims=True)
    pc = jnp.where(cmask, jnp.exp2((scm - mc) * SM_SCALE_LOG2), 0.0)
    lc = jnp.sum(pc, axis=1, keepdims=True)
    pc = pc * jnp.where(lc > 0.0, 1.0 / jnp.maximum(lc, 1e-30), 0.0)
    o_cmp = jnp.dot(pc.astype(BF16), vc_ref[...], preferred_element_type=F32)

    pcs = pc[0:QB]
    for h in range(1, NSA_HPG):
        pcs = pcs + pc[h * QB:(h + 1) * QB]
    imp_m = imp_ref[...]
    p_hi = pcs.astype(BF16)
    r1 = pcs - p_hi.astype(F32)
    p_mid = r1.astype(BF16)
    p_lo = (r1 - p_mid.astype(F32)).astype(BF16)
    imp = (jnp.dot(p_hi, imp_m, preferred_element_type=F32)
           + jnp.dot(p_mid, imp_m, preferred_element_type=F32)
           + jnp.dot(p_lo, imp_m, preferred_element_type=F32))

    lane_i = lax.broadcasted_iota(I32, (QB, LANES), 1)
    lane = lane_i.astype(F32)
    cur_blk = t_q // SEL_LEN
    valid = lane_i * SEL_LEN <= t_q
    forced = (lane_i == 0) | (lane_i == cur_blk) | (lane_i == cur_blk - 1)
    score = jnp.where(valid, jnp.where(forced, jnp.inf, imp), -1.0)
    sel = jnp.zeros((QB, LANES), F32)
    for _ in range(N_SELECT):
        m = jnp.max(score, axis=1, keepdims=True)
        idx = jnp.min(jnp.where(score == m, lane, float(LANES)), axis=1, keepdims=True)
        hit = lane == idx
        sel = jnp.where(hit, 1.0, sel)
        score = jnp.where(hit, -2.0, score)
    sel_b = sel.astype(BF16)

    acc[...] = jnp.zeros_like(acc)
    blocks_per_tile = tk // SEL_LEN

    def sel_tile(kt, carry):
        m, l = carry
        off = pl.multiple_of(kt * tk, tk)
        k = ks_ref[pl.ds(off, tk), :]
        v = vs_ref[pl.ds(off, tk), :]
        s = lax.dot_general(qs, k, nt, preferred_element_type=F32)
        brow = lax.broadcasted_iota(I32, (LANES, tk), 0)
        kcol = lax.broadcasted_iota(I32, (LANES, tk), 1)
        expand = jnp.where(brow == kt * blocks_per_tile + kcol // SEL_LEN, 1.0, 0.0).astype(BF16)
        picked = jnp.dot(sel_b, expand, preferred_element_type=F32)
        kpos = off + lax.broadcasted_iota(I32, (QB, tk), 1)
        ok = (picked > 0.5) & (kpos <= t_q)
        s = s + stack(jnp.where(ok, 0.0, NEG))
        m_new = jnp.maximum(m, jnp.max(s, axis=1, keepdims=True))
        alpha = jnp.exp2((m - m_new) * SM_SCALE_LOG2)
        p = jnp.exp2((s - m_new) * SM_SCALE_LOG2)
        l_new = alpha * l + jnp.sum(p, axis=1, keepdims=True)
        acc[...] = alpha * acc[...] + jnp.dot(p.astype(BF16), v, preferred_element_type=F32)
        return m_new, l_new

    n_tiles = (s0 + QB + tk - 1) // tk
    _, l_sel = lax.fori_loop(0, n_tiles, sel_tile,
                             (jnp.full((rows, 1), NEG, F32), jnp.zeros((rows, 1), F32)))
    o_sel = acc[...] / l_sel

    WK = WINDOW + QB
    start = pl.multiple_of(jnp.maximum(s0 - WINDOW, 0), QB)
    kw = kw_ref[pl.ds(start, WK), :]
    vw = vw_ref[pl.ds(start, WK), :]
    sw = lax.dot_general(qs, kw, nt, preferred_element_type=F32)
    wpos = start + lax.broadcasted_iota(I32, (QB, WK), 1)
    wok = (wpos <= t_q) & (wpos > t_q - WINDOW)
    sw = sw + stack(jnp.where(wok, 0.0, NEG))
    mw = jnp.max(sw, axis=1, keepdims=True)
    pw = jnp.exp2((sw - mw) * SM_SCALE_LOG2)
    lw = jnp.sum(pw, axis=1, keepdims=True)
    o_win = jnp.dot(pw.astype(BF16), vw, preferred_element_type=F32) / lw

    gates = gates_ref[...]
    for h in range(NSA_HPG):
        r = slice(h * QB, (h + 1) * QB)
        y = (o_cmp[r] * gates[:, 3 * h:3 * h + 1] + o_sel[r] * gates[:, 3 * h + 1:3 * h + 2]
             + o_win[r] * gates[:, 3 * h + 2:3 * h + 3])
        o_ref[:, h * HEAD_DIM:(h + 1) * HEAD_DIM] = y.astype(o_ref.dtype)


def _nsa_attention(proj, kc, vc, gates, imp_m, B, S, tk=256):
    G = NSA_GROUPS
    QB = NSA_Q_BLOCK
    nq = S // QB
    T = B * S
    NC = S // CMP_STRIDE
    gw = NSA_HPG * HEAD_DIM
    kv = lambda base: pl.BlockSpec((S, HEAD_DIM), lambda b, g, q: (b, base + g))
    cblk = pl.BlockSpec((None, None, NC, HEAD_DIM), lambda b, g, q: (b, g, 0, 0))
    return pl.pallas_call(
        functools.partial(_nsa_attn_kernel, S, tk),
        out_shape=jax.ShapeDtypeStruct((T, NSA_HEADS * HEAD_DIM), BF16),
        grid=(B, G, nq),
        in_specs=[
            pl.BlockSpec((QB, gw), lambda b, g, q: (b * nq + q, g)),
            cblk, cblk,
            kv(4 * G), kv(5 * G), kv(6 * G), kv(7 * G),
            pl.BlockSpec((QB, LANES), lambda b, g, q: (b * nq + q, g)),
            pl.BlockSpec((NC, LANES), lambda b, g, q: (0, 0)),
        ],
        out_specs=pl.BlockSpec((QB, gw), lambda b, g, q: (b * nq + q, g)),
        scratch_shapes=[pltpu.VMEM((NSA_HPG * QB, HEAD_DIM), F32)],
        compiler_params=_cparams(("arbitrary", "arbitrary", "arbitrary")),
        name="nsa_attn",
    )(proj, kc, vc, proj, proj, proj, proj, gates, imp_m)


def _importance_matrix(NC):
    R = SEL_LEN // CMP_STRIDE
    C = CMP_LEN // CMP_STRIDE
    i = jnp.arange(NC, dtype=I32)[:, None]
    j = jnp.arange(LANES, dtype=I32)[None, :]
    w = jnp.zeros((NC, LANES), F32)
    for m in range(R):
        for n in range(C):
            w = w + jnp.where((i == R * j + m - n) & (i < NC - 1), 1.0, 0.0)
    return w.astype(BF16)


def _nsa_mixer(h_bf, cos2, sin2, w_in, ck_pos, ck_w1, ck_w2, cv_pos, cv_w1, cv_w2, B, S):
    D = D_MODEL
    G = NSA_GROUPS
    q_w = w_in[:, :D]
    kvw = w_in[:, D:D + 6 * NSA_KV].reshape(D, N_BRANCH, 2, NSA_KV)
    g_w = w_in[:, D + 6 * NSA_KV:].reshape(D, G, NSA_HPG * N_BRANCH)
    w_main = jnp.concatenate([q_w, kvw[:, 1, 0], kvw[:, 2, 0], kvw[:, 1, 1], kvw[:, 2, 1]], axis=1).astype(BF16)
    w_cmp = jnp.concatenate([kvw[:, 0, 0], kvw[:, 0, 1]], axis=1).astype(BF16)
    w_gate = jnp.pad(g_w, ((0, 0), (0, 0), (0, LANES - NSA_HPG * N_BRANCH))).reshape(D, G * LANES).astype(BF16)

    proj = _proj(h_bf, w_main, cos2, sin2, (D + 2 * NSA_KV) // 512, BF16)
    cmp = _proj(h_bf, w_cmp, cos2, sin2, NSA_KV // 512, F32)
    gates = _gate_proj(h_bf, w_gate)

    NC = S // CMP_STRIDE
    W = CMP_STRIDE * HEAD_DIM

    def chunks(t):
        return t.reshape(B, NC, CMP_STRIDE, G, HEAD_DIM).transpose(0, 3, 1, 2, 4).reshape(B, G, NC, W)

    def halves(w1):
        return w1[:W].astype(BF16), w1[W:].astype(BF16)

    k1t, k1b = halves(ck_w1)
    v1t, v1b = halves(cv_w1)
    kc, vc = _compress(chunks(cmp[:, :NSA_KV]), chunks(cmp[:, NSA_KV:]),
                       ck_pos.reshape(2, W), cv_pos.reshape(2, W),
                       k1t, k1b, ck_w2.astype(BF16), v1t, v1b, cv_w2.astype(BF16))
    return _nsa_attention(proj, kc, vc, gates, _importance_matrix(NC), B, S)


def kernel(x, positions, ln_gain, ln_bias, diff_w_in, diff_lambda_q1, diff_lambda_k1, diff_lambda_q2, diff_lambda_k2, diff_subln_gain, diff_w_o, nsa_w_in, nsa_cmp_k_pos, nsa_cmp_k_w1, nsa_cmp_k_w2, nsa_cmp_v_pos, nsa_cmp_v_w1, nsa_cmp_v_w2, nsa_w_o, moe_router_w, moe_router_b, moe_w_gate_up, moe_b_gate_up, moe_w_down, moe_b_down):
    B, S, D = x.shape
    T = B * S
    assert D == D_MODEL and S % 512 == 0 and S // SEL_LEN <= LANES and S >= WINDOW + NSA_Q_BLOCK

    inv = ROPE_THETA ** (-jnp.arange(0, HEAD_DIM, 2, dtype=F32) / HEAD_DIM)
    ang = positions.astype(F32).reshape(T, 1) * inv[None, :]
    cos2 = jnp.concatenate([jnp.cos(ang), jnp.cos(ang)], axis=1)
    sin2 = jnp.concatenate([-jnp.sin(ang), jnp.sin(ang)], axis=1)

    w_gu = moe_w_gate_up.astype(BF16)
    w_dn = moe_w_down.astype(BF16)
    b_gu = moe_b_gate_up.reshape(DEPTH, N_EXPERTS, 1, 2 * D_FF)
    b_dn = moe_b_down.reshape(DEPTH, N_EXPERTS, 1, D)
    rw = jnp.pad(moe_router_w, ((0, 0), (0, 0), (0, LANES - N_EXPERTS)))
    rw_hi = rw.astype(BF16)
    rw_lo = (rw - rw_hi.astype(F32)).astype(BF16)
    rb = jnp.pad(moe_router_b, ((0, 0), (0, LANES - N_EXPERTS))).reshape(DEPTH, 1, LANES)

    h = x.reshape(T, D)
    h_bf = h.astype(BF16)
    for li in range(DEPTH):
        j = li // 2
        if li % 2 == 0:
            lambda_init = 0.8 - 0.6 * math.exp(-0.3 * li)
            proj = _proj(h_bf, diff_w_in[j].astype(BF16), cos2, sin2,
                         4 * DIFF_HEADS * HEAD_DIM // 512, BF16)
            lamv = jnp.stack([diff_lambda_q1[j], diff_lambda_k1[j], diff_lambda_q2[j], diff_lambda_k2[j]])
            mix = _diff_attention(proj, lamv, diff_subln_gain[j].reshape(1, DIFF_V_DIM), lambda_init, B, S)
            w_o = diff_w_o[j].astype(BF16)
        else:
            mix = _nsa_mixer(h_bf, cos2, sin2, nsa_w_in[j], nsa_cmp_k_pos[j], nsa_cmp_k_w1[j], nsa_cmp_k_w2[j],
                             nsa_cmp_v_pos[j], nsa_cmp_v_w1[j], nsa_cmp_v_w2[j], B, S)
            w_o = nsa_w_o[j].astype(BF16)
        h, topi, gate = _out_ln_route(mix, w_o, h, ln_gain[li, 0].reshape(1, D), ln_bias[li, 0].reshape(1, D),
                                      rw_hi[li], rw_lo[li], rb[li])
        h, h_bf = _moe_layer(h, topi, gate, li, w_gu, b_gu, w_dn, b_dn,
                             ln_gain[li, 1].reshape(1, D), ln_bias[li, 1].reshape(1, D))
    return h.reshape(B, S, D)
```

```python
import functools
import math

import jax
import jax.numpy as jnp
from jax import lax
from jax.experimental import pallas as pl
from jax.experimental.pallas import tpu as pltpu

F32 = jnp.float32
BF16 = jnp.bfloat16
I32 = jnp.int32

D_MODEL = 2048
DEPTH = 2
HEAD_DIM = 128
ROPE_THETA = 10000.0

DIFF_HEADS = D_MODEL // (2 * HEAD_DIM)
DIFF_V_DIM = 2 * HEAD_DIM

NSA_HEADS = D_MODEL // HEAD_DIM
NSA_GROUPS = 4
NSA_HPG = NSA_HEADS // NSA_GROUPS
NSA_KV = NSA_GROUPS * HEAD_DIM
N_BRANCH = 3
CMP_LEN = 32
CMP_STRIDE = 16
SEL_LEN = 64
N_SELECT = 16
WINDOW = 512
NSA_Q_BLOCK = 128

N_EXPERTS = 32
TOP_K = 4
D_FF = D_MODEL
SWIGLU_LIMIT = 7.0
SWIGLU_ALPHA = 1.702

DEEPNORM_ALPHA = (2 * DEPTH) ** 0.25
LN_EPS = 1e-5
RMS_EPS = 1e-5
NEG = -1e30
LOG2E = 1.4426950408889634
SM_SCALE_LOG2 = (HEAD_DIM ** -0.5) * LOG2E

LANES = 128
VMEM_LIMIT = 56 * 1024 * 1024
MOE_ROWS = 256


def _cparams(sem):
    return pltpu.CompilerParams(dimension_semantics=sem, vmem_limit_bytes=VMEM_LIMIT)


def _proj_kernel(n_rope_tiles, a_ref, w_ref, cos_ref, sin_ref, o_ref):
    j = pl.program_id(0)
    y = jnp.dot(a_ref[...], w_ref[...], preferred_element_type=F32)
    tn = y.shape[1]

    @pl.when(j < n_rope_tiles)
    def _():
        cos = cos_ref[...]
        sin = sin_ref[...]
        for c in range(tn // HEAD_DIM):
            t = y[:, c * HEAD_DIM:(c + 1) * HEAD_DIM]
            r = t * cos + pltpu.roll(t, HEAD_DIM // 2, 1) * sin
            o_ref[:, c * HEAD_DIM:(c + 1) * HEAD_DIM] = r.astype(o_ref.dtype)

    @pl.when(j >= n_rope_tiles)
    def _():
        o_ref[...] = y.astype(o_ref.dtype)


def _proj(a, w, cos2, sin2, n_rope_tiles, out_dtype, tm=1024, tn=512):
    M, K = a.shape
    N = w.shape[1]
    tm = min(tm, M)
    return pl.pallas_call(
        functools.partial(_proj_kernel, n_rope_tiles),
        out_shape=jax.ShapeDtypeStruct((M, N), out_dtype),
        grid=(N // tn, M // tm),
        in_specs=[
            pl.BlockSpec((tm, K), lambda j, i: (i, 0)),
            pl.BlockSpec((K, tn), lambda j, i: (0, j)),
            pl.BlockSpec((tm, HEAD_DIM), lambda j, i: (i, 0)),
            pl.BlockSpec((tm, HEAD_DIM), lambda j, i: (i, 0)),
        ],
        out_specs=pl.BlockSpec((tm, tn), lambda j, i: (i, j)),
        compiler_params=_cparams(("arbitrary", "arbitrary")),
        name="proj_rope",
    )(a, w, cos2, sin2)


def _gate_kernel(a_ref, w_ref, o_ref):
    y = jnp.dot(a_ref[...], w_ref[...], preferred_element_type=F32)
    o_ref[...] = jax.nn.sigmoid(y)


def _gate_proj(a, w, tm=1024):
    M, K = a.shape
    N = w.shape[1]
    tm = min(tm, M)
    return pl.pallas_call(
        _gate_kernel,
        out_shape=jax.ShapeDtypeStruct((M, N), F32),
        grid=(M // tm,),
        in_specs=[pl.BlockSpec((tm, K), lambda i: (i, 0)),
                  pl.BlockSpec((K, N), lambda i: (0, 0))],
        out_specs=pl.BlockSpec((tm, N), lambda i: (i, 0)),
        compiler_params=_cparams(("arbitrary",)),
        name="nsa_gates",
    )(a, w)


def _diff_attn_kernel(lambda_init, tq, rc, tk, q1_ref, q2_ref, k1_ref, k2_ref, v_ref,
                      lam_ref, g_ref, o_ref, acc1, acc2):
    qi = pl.program_id(2)
    lamv = lam_ref[...]
    lam = (jnp.exp(jnp.sum(lamv[0:1] * lamv[1:2], axis=1, keepdims=True))
           - jnp.exp(jnp.sum(lamv[2:3] * lamv[3:4], axis=1, keepdims=True)) + lambda_init)
    gain = g_ref[...]

    def chunk(r, c0):
        r0 = pl.multiple_of(r * rc, rc)
        s0 = qi * tq + r0
        q1 = q1_ref[pl.ds(r0, rc), :]
        q2 = q2_ref[pl.ds(r0, rc), :]
        acc1[...] = jnp.zeros_like(acc1)
        acc2[...] = jnp.zeros_like(acc2)

        def tile(kt, carry, masked):
            off = pl.multiple_of(kt * tk, tk)
            v = v_ref[pl.ds(off, tk), :]

            def one(q, k_ref, m, l, acc):
                k = k_ref[pl.ds(off, tk), :]
                s = lax.dot_general(q, k, (((1,), (1,)), ((), ())), preferred_element_type=F32)
                if masked:
                    row = s0 + lax.broadcasted_iota(I32, s.shape, 0)
                    col = off + lax.broadcasted_iota(I32, s.shape, 1)
                    s = jnp.where(col <= row, s, NEG)
                m_new = jnp.maximum(m, jnp.max(s, axis=1, keepdims=True))
                alpha = jnp.exp2((m - m_new) * SM_SCALE_LOG2)
                p = jnp.exp2((s - m_new) * SM_SCALE_LOG2)
                l_new = alpha * l + jnp.sum(p, axis=1, keepdims=True)
                acc[...] = alpha * acc[...] + jnp.dot(p.astype(BF16), v, preferred_element_type=F32)
                return m_new, l_new

            m1, l1, m2, l2 = carry
            m1, l1 = one(q1, k1_ref, m1, l1, acc1)
            m2, l2 = one(q2, k2_ref, m2, l2, acc2)
            return m1, l1, m2, l2

        neg = jnp.full((rc, 1), NEG, F32)
        zero = jnp.zeros((rc, 1), F32)
        n_full = s0 // tk
        carry = lax.fori_loop(0, n_full, lambda kt, c: tile(kt, c, False), (neg, zero, neg, zero))
        for j in range(max(1, rc // tk)):
            carry = tile(n_full + j, carry, True)
        _, l1, _, l2 = carry

        a = acc1[...] / l1 - lam * (acc2[...] / l2)
        ms = jnp.mean(a * a, axis=1, keepdims=True)
        o = a * lax.rsqrt(ms + RMS_EPS) * gain
        o_ref[pl.ds(r0, rc), :] = (o * (1.0 - lambda_init)).astype(o_ref.dtype)
        return c0

    lax.fori_loop(0, tq // rc, chunk, 0)


def _diff_attn_t_kernel(lambda_init, tq, tk, q1_ref, q2_ref, k1_ref, k2_ref, vt_ref,
                        lam_ref, g_ref, o_ref, acc1, acc2, s1_buf, s2_buf):
    qi = pl.program_id(2)
    s0 = qi * tq
    nt = (((1,), (1,)), ((), ()))
    lamv = lam_ref[...]
    lam = (jnp.exp(jnp.sum(lamv[0:1] * lamv[1:2], axis=1, keepdims=True))
           - jnp.exp(jnp.sum(lamv[2:3] * lamv[3:4], axis=1, keepdims=True)) + lambda_init)
    acc1[...] = jnp.zeros_like(acc1)
    acc2[...] = jnp.zeros_like(acc2)

    def scores(kt, slot):
        off = pl.multiple_of(kt * tk, tk)
        s1_buf[slot] = lax.dot_general(k1_ref[pl.ds(off, tk), :], q1_ref[...], nt,
                                       preferred_element_type=F32)
        s2_buf[slot] = lax.dot_general(k2_ref[pl.ds(off, tk), :], q2_ref[...], nt,
                                       preferred_element_type=F32)

    def consume(kt, slot, carry, masked):
        off = pl.multiple_of(kt * tk, tk)
        vt = vt_ref[:, pl.ds(off, tk)]

        def one(s_buf, m, l, acc):
            st = s_buf[slot]
            if masked:
                kpos = off + lax.broadcasted_iota(I32, st.shape, 0)
                qpos = s0 + lax.broadcasted_iota(I32, st.shape, 1)
                st = jnp.where(kpos <= qpos, st, NEG)
            m_new = jnp.maximum(m, jnp.max(st, axis=0, keepdims=True))
            alpha = jnp.exp2((m - m_new) * SM_SCALE_LOG2)
            p = jnp.exp2((st - m_new) * SM_SCALE_LOG2)
            l_new = alpha * l + jnp.sum(p, axis=0, keepdims=True)
            acc[...] = alpha * acc[...] + jnp.dot(vt, p.astype(BF16), preferred_element_type=F32)
            return m_new, l_new

        m1, l1, m2, l2 = carry
        m1, l1 = one(s1_buf, m1, l1, acc1)
        m2, l2 = one(s2_buf, m2, l2, acc2)
        return m1, l1, m2, l2

    def step(kt, carry):
        slot = kt % 2
        scores(kt + 1, 1 - slot)
        return consume(kt, slot, carry, False)

    neg = jnp.full((1, tq), NEG, F32)
    zero = jnp.zeros((1, tq), F32)
    n_full = s0 // tk
    scores(0, 0)
    carry = lax.fori_loop(0, n_full, step, (neg, zero, neg, zero))
    carry = consume(n_full, n_full % 2, carry, True)
    for j in range(1, max(1, tq // tk)):
        scores(n_full + j, 0)
        carry = consume(n_full + j, 0, carry, True)
    _, l1, _, l2 = carry

    a = acc1[...] / l1 - lam * (acc2[...] / l2)
    ms = jnp.mean(a * a, axis=0, keepdims=True)
    o = a * lax.rsqrt(ms + RMS_EPS) * g_ref[...]
    o_ref[...] = (o * (1.0 - lambda_init)).T.astype(o_ref.dtype)


def _diff_attention_t(proj, lamv, subln_g, lambda_init, B, S, tq=512, tk=512):
    H = DIFF_HEADS
    tq = min(tq, S)
    nq = S // tq
    T = B * S
    assert tk % tq == 0 or tq % tk == 0
    v_t = proj[:, 4 * H * HEAD_DIM:].reshape(B, S, H, DIFF_V_DIM).transpose(0, 2, 3, 1)
    kern = functools.partial(_diff_attn_t_kernel, lambda_init, tq, tk)
    return pl.pallas_call(
        kern,
        out_shape=jax.ShapeDtypeStruct((T, H * DIFF_V_DIM), BF16),
        grid=(B, H, nq),
        in_specs=[
            pl.BlockSpec((tq, HEAD_DIM), lambda b, h, q: (b * nq + q, h)),
            pl.BlockSpec((tq, HEAD_DIM), lambda b, h, q: (b * nq + q, H + h)),
            pl.BlockSpec((S, HEAD_DIM), lambda b, h, q: (b, 2 * H + h)),
            pl.BlockSpec((S, HEAD_DIM), lambda b, h, q: (b, 3 * H + h)),
            pl.BlockSpec((None, None, DIFF_V_DIM, S), lambda b, h, q: (b, h, 0, 0)),
            pl.BlockSpec((4, HEAD_DIM), lambda b, h, q: (0, 0)),
            pl.BlockSpec((DIFF_V_DIM, 1), lambda b, h, q: (0, 0)),
        ],
        out_specs=pl.BlockSpec((tq, DIFF_V_DIM), lambda b, h, q: (b * nq + q, h)),
        scratch_shapes=[pltpu.VMEM((DIFF_V_DIM, tq), F32), pltpu.VMEM((DIFF_V_DIM, tq), F32),
                        pltpu.VMEM((2, tk, tq), F32), pltpu.VMEM((2, tk, tq), F32)],
        compiler_params=_cparams(("arbitrary", "arbitrary", "arbitrary")),
        name="diff_attn",
    )(proj, proj, proj, proj, v_t, lamv, subln_g.reshape(DIFF_V_DIM, 1))


def _diff_attention(proj, lamv, subln_g, lambda_init, B, S, tq=1024, rc=1024, tk=512):
    H = DIFF_HEADS
    tq = min(tq, S)
    nq = S // tq
    T = B * S
    assert (tk % rc == 0 or rc % tk == 0) and tq % rc == 0
    kern = functools.partial(_diff_attn_kernel, lambda_init, tq, rc, tk)
    return pl.pallas_call(
        kern,
        out_shape=jax.ShapeDtypeStruct((T, H * DIFF_V_DIM), BF16),
        grid=(B, H, nq),
        in_specs=[
            pl.BlockSpec((tq, HEAD_DIM), lambda b, h, q: (b * nq + q, h)),
            pl.BlockSpec((tq, HEAD_DIM), lambda b, h, q: (b * nq + q, H + h)),
            pl.BlockSpec((S, HEAD_DIM), lambda b, h, q: (b, 2 * H + h)),
            pl.BlockSpec((S, HEAD_DIM), lambda b, h, q: (b, 3 * H + h)),
            pl.BlockSpec((S, DIFF_V_DIM), lambda b, h, q: (b, 2 * H + h)),
            pl.BlockSpec((4, HEAD_DIM), lambda b, h, q: (0, 0)),
            pl.BlockSpec((1, DIFF_V_DIM), lambda b, h, q: (0, 0)),
        ],
        out_specs=pl.BlockSpec((tq, DIFF_V_DIM), lambda b, h, q: (b * nq + q, h)),
        scratch_shapes=[pltpu.VMEM((rc, DIFF_V_DIM), F32), pltpu.VMEM((rc, DIFF_V_DIM), F32)],
        compiler_params=_cparams(("arbitrary", "arbitrary", "arbitrary")),
        name="diff_attn",
    )(proj, proj, proj, proj, proj, lamv, subln_g)


def _layer_norm(z, g, b):
    mu = jnp.mean(z, axis=-1, keepdims=True)
    zc = z - mu
    var = jnp.mean(zc * zc, axis=-1, keepdims=True)
    return zc * lax.rsqrt(var + LN_EPS) * g + b


def _route(hn, rw_hi_ref, rw_lo_ref, rb_ref, topi_ref, gate_ref, rank_ref, counts_ref, run_ref):
    hi = hn.astype(BF16)
    lo = (hn - hi.astype(F32)).astype(BF16)
    w_hi = rw_hi_ref[...]
    logits = (jnp.dot(hi, w_hi, preferred_element_type=F32)
              + jnp.dot(lo, w_hi, preferred_element_type=F32)
              + jnp.dot(hi, rw_lo_ref[...], preferred_element_type=F32)) + rb_ref[...]
    lane = lax.broadcasted_iota(I32, logits.shape, 1).astype(F32)
    cur = jnp.where(lane < N_EXPERTS, logits, -jnp.inf)
    vals, idxs = [], []
    for _ in range(TOP_K):
        m = jnp.max(cur, axis=1, keepdims=True)
        idx = jnp.min(jnp.where(cur == m, lane, float(LANES)), axis=1, keepdims=True)
        vals.append(m)
        idxs.append(idx)
        cur = jnp.where(lane == idx, -jnp.inf, cur)
    es = [jnp.exp(v - vals[0]) for v in vals]
    den = es[0] + es[1] + es[2] + es[3]
    gate_out = jnp.zeros_like(logits)
    idx_out = jnp.zeros_like(logits)
    for k in range(TOP_K):
        gate_out = jnp.where(lane == float(k), es[k] / den, gate_out)
        idx_out = jnp.where(lane == float(k), idxs[k], idx_out)
    gate_ref[...] = gate_out
    topi_ref[...] = idx_out.astype(I32)

    @pl.when(pl.program_id(0) == 0)
    def _():
        run_ref[...] = jnp.zeros_like(run_ref)

    tm = logits.shape[0]
    onehot = jnp.zeros_like(logits)
    for k in range(TOP_K):
        onehot = jnp.where(lane == idxs[k], 1.0, onehot)
    r_i = lax.broadcasted_iota(I32, (tm, tm), 0)
    c_i = lax.broadcasted_iota(I32, (tm, tm), 1)
    tri = jnp.where(r_i > c_i, 1.0, 0.0).astype(BF16)
    pos = jnp.dot(tri, onehot.astype(BF16), preferred_element_type=F32) + run_ref[...]
    rank_out = jnp.zeros_like(logits)
    for k in range(TOP_K):
        rk = jnp.sum(jnp.where(lane == idxs[k], pos, 0.0), axis=1, keepdims=True)
        rank_out = jnp.where(lane == float(k), rk, rank_out)
    rank_ref[...] = rank_out.astype(I32)
    run_ref[...] = run_ref[...] + jnp.sum(onehot, axis=0, keepdims=True)
    counts_ref[...] = run_ref[...].astype(I32)


def _out_ln_kernel(a_ref, w_ref, h_ref, g_ref, b_ref, rw_hi_ref, rw_lo_ref, rb_ref,
                   o_ref, topi_ref, gate_ref, rank_ref, counts_ref, run_ref):
    mix = jnp.dot(a_ref[...], w_ref[...], preferred_element_type=F32)
    hn = _layer_norm(DEEPNORM_ALPHA * h_ref[...] + mix, g_ref[...], b_ref[...])
    o_ref[...] = hn
    _route(hn, rw_hi_ref, rw_lo_ref, rb_ref, topi_ref, gate_ref, rank_ref, counts_ref, run_ref)


def _out_ln_route(a, w_o, h, gain, bias, rw_hi, rw_lo, rb, tm=256):
    T, K = a.shape
    D = w_o.shape[1]
    tm = min(tm, T)
    row = lambda i: (i, 0)
    fixed = lambda i: (0, 0)
    return pl.pallas_call(
        _out_ln_kernel,
        out_shape=(jax.ShapeDtypeStruct((T, D), F32),
                   jax.ShapeDtypeStruct((T, LANES), I32),
                   jax.ShapeDtypeStruct((T, LANES), F32),
                   jax.ShapeDtypeStruct((T, LANES), I32),
                   jax.ShapeDtypeStruct((1, LANES), I32)),
        grid=(T // tm,),
        in_specs=[
            pl.BlockSpec((tm, K), row),
            pl.BlockSpec((K, D), fixed),
            pl.BlockSpec((tm, D), row),
            pl.BlockSpec((1, D), fixed),
            pl.BlockSpec((1, D), fixed),
            pl.BlockSpec((D, LANES), fixed),
            pl.BlockSpec((D, LANES), fixed),
            pl.BlockSpec((1, LANES), fixed),
        ],
        out_specs=(pl.BlockSpec((tm, D), row), pl.BlockSpec((tm, LANES), row),
                   pl.BlockSpec((tm, LANES), row), pl.BlockSpec((tm, LANES), row),
                   pl.BlockSpec((1, LANES), fixed)),
        scratch_shapes=[pltpu.VMEM((1, LANES), F32)],
        compiler_params=_cparams(("arbitrary",)),
        name="out_ln_route",
    )(a, w_o, h, gain, bias, rw_hi, rw_lo, rb)


def _dispatch_kernel(tmd, dest_ref, fill_ref, h_ref, xout_ref, zrow, sem):
    def row_copy(t, d):
        return pltpu.make_async_copy(h_ref.at[pl.ds(t, 1), :], xout_ref.at[pl.ds(d, 1), :], sem)

    def zero_copy(d):
        return pltpu.make_async_copy(zrow.at[pl.ds(0, 1), :], xout_ref.at[pl.ds(d, 1), :], sem)

    def zero_block(b):
        r0 = pl.multiple_of(b * MOE_ROWS, MOE_ROWS)
        return pltpu.make_async_copy(zrow, xout_ref.at[pl.ds(r0, MOE_ROWS), :], sem)

    @pl.when(pl.program_id(0) == 0)
    def _():
        zrow[...] = jnp.zeros_like(zrow)
        n_used = fill_ref[0, N_EXPERTS]
        n_blocks = xout_ref.shape[0] // MOE_ROWS

        def per_expert(e, c):
            lax.fori_loop(fill_ref[0, e], fill_ref[1, e], lambda d, cc: (zero_copy(d).start(), cc)[1], 0)
            return c

        def per_expert_wait(e, c):
            lax.fori_loop(fill_ref[0, e], fill_ref[1, e], lambda d, cc: (zero_copy(d).wait(), cc)[1], 0)
            return c

        lax.fori_loop(0, N_EXPERTS, per_expert, 0)
        lax.fori_loop(n_used, n_blocks, lambda b, cc: (zero_block(b).start(), cc)[1], 0)
        lax.fori_loop(0, N_EXPERTS, per_expert_wait, 0)
        lax.fori_loop(n_used, n_blocks, lambda b, cc: (zero_block(b).wait(), cc)[1], 0)

    def issue(t, c):
        for k in range(TOP_K):
            row_copy(t, dest_ref[t * TOP_K + k]).start()
        return c

    def drain(t, c):
        for k in range(TOP_K):
            row_copy(t, dest_ref[t * TOP_K + k]).wait()
        return c

    lax.fori_loop(0, tmd, issue, 0)
    lax.fori_loop(0, tmd, drain, 0)


def _dispatch(h, dest_flat, fill, n_pad, tmd=256):
    T, D = h.shape
    tmd = min(tmd, T)
    return pl.pallas_call(
        functools.partial(_dispatch_kernel, tmd),
        out_shape=jax.ShapeDtypeStruct((n_pad, D), h.dtype),
        grid=(T // tmd,),
        in_specs=[
            pl.BlockSpec((tmd * TOP_K,), lambda i: (i,), memory_space=pltpu.SMEM),
            pl.BlockSpec(memory_space=pltpu.SMEM),
            pl.BlockSpec((tmd, D), lambda i: (i, 0)),
        ],
        out_specs=pl.BlockSpec(memory_space=pl.ANY),
        scratch_shapes=[pltpu.VMEM((MOE_ROWS, D), h.dtype), pltpu.SemaphoreType.DMA(())],
        compiler_params=_cparams(("arbitrary",)),
        name="moe_dispatch",
    )(dest_flat, fill, h)


def _fresh_weights(be_ref, nu_ref):
    i = pl.program_id(1)
    changed = (i == 0) | (be_ref[i] != be_ref[jnp.maximum(i - 1, 0)])
    return changed & (i < nu_ref[0])


def _moe_up_kernel(be_ref, nu_ref, x_ref, wg_ref, wu_ref, bg_ref, bu_ref, o_ref, wg_bf, wu_bf):
    i = pl.program_id(1)

    @pl.when(_fresh_weights(be_ref, nu_ref))
    def _():
        wg_bf[...] = wg_ref[...].astype(BF16)
        wu_bf[...] = wu_ref[...].astype(BF16)

    @pl.when(i < nu_ref[0])
    def _():
        x = x_ref[...].astype(BF16)
        g = jnp.dot(x, wg_bf[...], preferred_element_type=F32) + bg_ref[...]
        u = jnp.dot(x, wu_bf[...], preferred_element_type=F32) + bu_ref[...]
        g = jnp.minimum(g, SWIGLU_LIMIT)
        u = jnp.clip(u, -SWIGLU_LIMIT, SWIGLU_LIMIT)
        act = g * jax.nn.sigmoid(SWIGLU_ALPHA * g) * (u + 1.0)
        o_ref[...] = act.astype(o_ref.dtype)

    @pl.when(i >= nu_ref[0])
    def _():
        o_ref[...] = jnp.zeros_like(o_ref)


def _moe_up(x_rows, w_gu, b_gu, block_e, n_used, li, tf=1024):
    n_pad, D = x_rows.shape
    R = MOE_ROWS
    nb = n_pad // R
    nf = D_FF // tf

    def xmap(j, i, be, nu):
        return (jnp.minimum(i, nu[0] - 1), 0)

    grid_spec = pltpu.PrefetchScalarGridSpec(
        num_scalar_prefetch=2,
        grid=(nf, nb),
        in_specs=[
            pl.BlockSpec((R, D), xmap),
            pl.BlockSpec((None, None, D, tf), lambda j, i, be, nu: (li, be[i], 0, j)),
            pl.BlockSpec((None, None, D, tf), lambda j, i, be, nu: (li, be[i], 0, nf + j)),
            pl.BlockSpec((None, None, 1, tf), lambda j, i, be, nu: (li, be[i], 0, j)),
            pl.BlockSpec((None, None, 1, tf), lambda j, i, be, nu: (li, be[i], 0, nf + j)),
        ],
        out_specs=pl.BlockSpec((R, tf), lambda j, i, be, nu: (i, j)),
        scratch_shapes=[pltpu.VMEM((D, tf), BF16), pltpu.VMEM((D, tf), BF16)],
    )
    return pl.pallas_call(
        _moe_up_kernel,
        out_shape=jax.ShapeDtypeStruct((n_pad, D_FF), BF16),
        grid_spec=grid_spec,
        compiler_params=_cparams(("arbitrary", "arbitrary")),
        name="moe_up",
    )(block_e, n_used, x_rows, w_gu, w_gu, b_gu, b_gu)


def _moe_down_kernel(be_ref, nu_ref, a_ref, w_ref, b_ref, o_ref, w_bf):
    i = pl.program_id(1)

    @pl.when(_fresh_weights(be_ref, nu_ref))
    def _():
        w_bf[...] = w_ref[...].astype(BF16)

    @pl.when(i < nu_ref[0])
    def _():
        o_ref[...] = jnp.dot(a_ref[...], w_bf[...], preferred_element_type=F32) + b_ref[...]

    @pl.when(i >= nu_ref[0])
    def _():
        o_ref[...] = jnp.zeros_like(o_ref)


def _moe_down(act, w_down, b_down, block_e, n_used, li, tn=1024):
    n_pad, F = act.shape
    R = MOE_ROWS
    nb = n_pad // R
    D = D_MODEL

    def amap(j, i, be, nu):
        return (jnp.minimum(i, nu[0] - 1), 0)

    grid_spec = pltpu.PrefetchScalarGridSpec(
        num_scalar_prefetch=2,
        grid=(D // tn, nb),
        in_specs=[
            pl.BlockSpec((R, F), amap),
            pl.BlockSpec((None, None, F, tn), lambda j, i, be, nu: (li, be[i], 0, j)),
            pl.BlockSpec((None, None, 1, tn), lambda j, i, be, nu: (li, be[i], 0, j)),
        ],
        out_specs=pl.BlockSpec((R, tn), lambda j, i, be, nu: (i, j)),
        scratch_shapes=[pltpu.VMEM((F, tn), BF16)],
    )
    return pl.pallas_call(
        _moe_down_kernel,
        out_shape=jax.ShapeDtypeStruct((n_pad, D), F32),
        grid_spec=grid_spec,
        compiler_params=_cparams(("arbitrary", "arbitrary")),
        name="moe_down",
    )(block_e, n_used, act, w_down, b_down)


def _combine_ln_kernel(tmc, dest_ref, y_ref, gate_ref, h_ref, g_ref, b_ref,
                       o_ref, obf_ref, buf, sem):
    def row_copy(t, k, d):
        return pltpu.make_async_copy(y_ref.at[pl.ds(d, 1), :], buf.at[k, pl.ds(t, 1), :], sem)

    def issue(t, c):
        for k in range(TOP_K):
            row_copy(t, k, dest_ref[t * TOP_K + k]).start()
        return c

    def drain(t, c):
        for k in range(TOP_K):
            row_copy(t, k, dest_ref[t * TOP_K + k]).wait()
        return c

    lax.fori_loop(0, tmc, issue, 0)
    lax.fori_loop(0, tmc, drain, 0)

    gate = gate_ref[...]
    ffn = gate[:, 0:1] * buf[0]
    for k in range(1, TOP_K):
        ffn = ffn + gate[:, k:k + 1] * buf[k]
    hn = _layer_norm(DEEPNORM_ALPHA * h_ref[...] + ffn, g_ref[...], b_ref[...])
    o_ref[...] = hn
    obf_ref[...] = hn.astype(BF16)


def _combine_ln(y_rows, dest_flat, gate, h, gain, bias, tmc=256):
    T, D = h.shape
    tmc = min(tmc, T)
    row = lambda i: (i, 0)
    fixed = lambda i: (0, 0)
    return pl.pallas_call(
        functools.partial(_combine_ln_kernel, tmc),
        out_shape=(jax.ShapeDtypeStruct((T, D), F32), jax.ShapeDtypeStruct((T, D), BF16)),
        grid=(T // tmc,),
        in_specs=[
            pl.BlockSpec((tmc * TOP_K,), lambda i: (i,), memory_space=pltpu.SMEM),
            pl.BlockSpec(memory_space=pl.ANY),
            pl.BlockSpec((tmc, LANES), row),
            pl.BlockSpec((tmc, D), row),
            pl.BlockSpec((1, D), fixed),
            pl.BlockSpec((1, D), fixed),
        ],
        out_specs=(pl.BlockSpec((tmc, D), row), pl.BlockSpec((tmc, D), row)),
        scratch_shapes=[pltpu.VMEM((TOP_K, tmc, D), F32), pltpu.SemaphoreType.DMA(())],
        compiler_params=_cparams(("arbitrary",)),
        name="moe_combine_ln",
    )(dest_flat, y_rows, gate, h, gain, bias)


def _moe_layer(h, topi, gate, rank, counts, li, w_gu, b_gu, w_down, b_down, gain, bias):
    T = h.shape[0]
    R = MOE_ROWS
    n_rows = T * TOP_K
    n_blocks = -(-(n_rows + N_EXPERTS * (R - 1)) // R)
    n_pad = n_blocks * R
    top_idx = topi[:, :TOP_K]
    counts = counts[0, :N_EXPERTS]
    padded = ((counts + R - 1) // R) * R
    pad_end = jnp.cumsum(padded)
    pad_start = pad_end - padded
    dest = (pad_start[top_idx] + rank[:, :TOP_K]).astype(I32).reshape(-1)
    block_e = jnp.minimum(
        jnp.searchsorted(pad_end, jnp.arange(n_blocks, dtype=I32) * R, side="right"),
        N_EXPERTS - 1).astype(I32)
    n_used = (pad_end[-1:] // R).astype(I32)
    fill = jnp.stack([jnp.concatenate([pad_start + counts, n_used]),
                      jnp.concatenate([pad_end, n_used])]).astype(I32)

    x_rows = _dispatch(h, dest, fill, n_pad)
    act = _moe_up(x_rows, w_gu, b_gu, block_e, n_used, li)
    y_rows = _moe_down(act, w_down, b_down, block_e, n_used, li)
    return _combine_ln(y_rows, dest, gate, h, gain, bias)


def _compress_one(c_ref, pos_ref, w1t_ref, w1b_ref, w2_ref, o_ref):
    c = c_ref[...]
    pos = pos_ref[...]
    a = jnp.dot((c + pos[0:1]).astype(BF16), w1t_ref[...], preferred_element_type=F32)
    b = jnp.dot((c + pos[1:2]).astype(BF16), w1b_ref[...], preferred_element_type=F32)
    n = a.shape[0]
    pre = a + pltpu.roll(b, n - 1, 0)
    hid = pre * jax.nn.sigmoid(pre)
    out = jnp.dot(hid.astype(BF16), w2_ref[...], preferred_element_type=F32)
    row = lax.broadcasted_iota(I32, out.shape, 0)
    o_ref[...] = jnp.where(row < n - 1, out, 0.0).astype(o_ref.dtype)


def _compress_kernel(ck_ref, cv_ref, pk_ref, pv_ref, k1t, k1b, k2, v1t, v1b, v2, ok_ref, ov_ref):
    _compress_one(ck_ref, pk_ref, k1t, k1b, k2, ok_ref)
    _compress_one(cv_ref, pv_ref, v1t, v1b, v2, ov_ref)


def _compress(ck, cv, pk, pv, k1t, k1b, k2, v1t, v1b, v2):
    B, G, NC, W = ck.shape
    blk = pl.BlockSpec((None, None, NC, W), lambda b, g: (b, g, 0, 0))
    oblk = pl.BlockSpec((None, None, NC, HEAD_DIM), lambda b, g: (b, g, 0, 0))
    full = lambda shape: pl.BlockSpec(shape, lambda b, g: (0, 0))
    return pl.pallas_call(
        _compress_kernel,
        out_shape=(jax.ShapeDtypeStruct((B, G, NC, HEAD_DIM), BF16),
                   jax.ShapeDtypeStruct((B, G, NC, HEAD_DIM), BF16)),
        grid=(B, G),
        in_specs=[blk, blk, full((2, W)), full((2, W)),
                  full((W, HEAD_DIM)), full((W, HEAD_DIM)), full((HEAD_DIM, HEAD_DIM)),
                  full((W, HEAD_DIM)), full((W, HEAD_DIM)), full((HEAD_DIM, HEAD_DIM))],
        out_specs=(oblk, oblk),
        compiler_params=_cparams(("arbitrary", "arbitrary")),
        name="nsa_compress",
    )(ck, cv, pk, pv, k1t, k1b, k2, v1t, v1b, v2)


def _nsa_attn_kernel(S, QB, tk, q_ref, kc_ref, vc_ref, ks_ref, kw_ref, vs_ref, vw_ref,
                     oh_ref, gates_ref, imp_ref, o_ref, acc, qa):
    qi = pl.program_id(2)
    NC = S // CMP_STRIDE
    s0 = qi * QB
    nt = (((1,), (1,)), ((), ()))

    q = q_ref[...]
    qs = jnp.concatenate([q[:, h * HEAD_DIM:(h + 1) * HEAD_DIM] for h in range(NSA_HPG)], axis=0)
    rows = NSA_HPG * QB
    t_q = s0 + lax.broadcasted_iota(I32, (QB, 1), 0)

    def stack(m):
        return jnp.concatenate([m] * NSA_HPG, axis=0)

    sc = lax.dot_general(qs, kc_ref[...], nt, preferred_element_type=F32)
    ncol = lax.broadcasted_iota(I32, (QB, NC), 1)
    cmask = stack(ncol * CMP_STRIDE + (CMP_LEN - 1) <= t_q)
    scm = jnp.where(cmask, sc, NEG)
    mc = jnp.max(scm, axis=1, keepdims=True)
    pc = jnp.where(cmask, jnp.exp2((scm - mc) * SM_SCALE_LOG2), 0.0)
    lc = jnp.sum(pc, axis=1, keepdims=True)
    pc = pc * jnp.where(lc > 0.0, 1.0 / jnp.maximum(lc, 1e-30), 0.0)
    o_cmp = jnp.dot(pc.astype(BF16), vc_ref[...], preferred_element_type=F32)

    pcs = pc[0:QB]
    for h in range(1, NSA_HPG):
        pcs = pcs + pc[h * QB:(h + 1) * QB]
    imp_m = imp_ref[...]
    p_hi = pcs.astype(BF16)
    r1 = pcs - p_hi.astype(F32)
    p_mid = r1.astype(BF16)
    p_lo = (r1 - p_mid.astype(F32)).astype(BF16)
    imp = (jnp.dot(p_hi, imp_m, preferred_element_type=F32)
           + jnp.dot(p_mid, imp_m, preferred_element_type=F32)
           + jnp.dot(p_lo, imp_m, preferred_element_type=F32))

    imp_t = imp.T
    blk_i = lax.broadcasted_iota(I32, (LANES, QB), 0)
    blk = blk_i.astype(F32)
    t_row = s0 + lax.broadcasted_iota(I32, (1, QB), 1)
    cur_blk = t_row // SEL_LEN
    valid = blk_i * SEL_LEN <= t_row
    forced = (blk_i == 0) | (blk_i == cur_blk) | (blk_i == cur_blk - 1)
    score = jnp.where(valid, jnp.where(forced, jnp.inf, imp_t), -1.0)
    sel_t = jnp.zeros((LANES, QB), F32)
    for _ in range(N_SELECT):
        m = jnp.max(score, axis=0, keepdims=True)
        idx = jnp.min(jnp.where(score == m, blk, float(LANES)), axis=0, keepdims=True)
        hit = blk == idx
        sel_t = jnp.where(hit, 1.0, sel_t)
        score = jnp.where(hit, -2.0, score)
    sel_bias = ((sel_t - 1.0) * (-NEG)).T.astype(BF16)
    qa[:, 0:HEAD_DIM] = qs
    qa[:, HEAD_DIM:2 * HEAD_DIM] = stack(sel_bias)

    acc[...] = jnp.zeros_like(acc)

    def sel_tile(kt, carry, causal):
        m, l = carry
        off = pl.multiple_of(kt * tk, tk)
        k = jnp.concatenate([ks_ref[pl.ds(off, tk), :], oh_ref[pl.ds(off, tk), :]], axis=1)
        v = vs_ref[pl.ds(off, tk), :]
        s = lax.dot_general(qa[...], k, nt, preferred_element_type=F32)
        if causal:
            kpos = off + lax.broadcasted_iota(I32, (QB, tk), 1)
            s = s + stack(jnp.where(kpos <= t_q, 0.0, NEG))
        m_new = jnp.maximum(m, jnp.max(s, axis=1, keepdims=True))
        alpha = jnp.exp2((m - m_new) * SM_SCALE_LOG2)
        p = jnp.exp2((s - m_new) * SM_SCALE_LOG2)
        l_new = alpha * l + jnp.sum(p, axis=1, keepdims=True)
        acc[...] = alpha * acc[...] + jnp.dot(p.astype(BF16), v, preferred_element_type=F32)
        return m_new, l_new

    n_full = s0 // tk
    carry = lax.fori_loop(0, n_full, lambda kt, c: sel_tile(kt, c, False),
                          (jnp.full((rows, 1), NEG, F32), jnp.zeros((rows, 1), F32)))
    _, l_sel = sel_tile(n_full, carry, True)
    o_sel = acc[...] / l_sel

    WK = WINDOW + QB
    start = pl.multiple_of(jnp.maximum(s0 - WINDOW, 0), NSA_Q_BLOCK)
    kw = kw_ref[pl.ds(start, WK), :]
    vw = vw_ref[pl.ds(start, WK), :]
    sw = lax.dot_general(qs, kw, nt, preferred_element_type=F32)
    wpos = start + lax.broadcasted_iota(I32, (QB, WK), 1)
    wok = (wpos <= t_q) & (wpos > t_q - WINDOW)
    sw = sw + stack(jnp.where(wok, 0.0, NEG))
    mw = jnp.max(sw, axis=1, keepdims=True)
    pw = jnp.exp2((sw - mw) * SM_SCALE_LOG2)
    lw = jnp.sum(pw, axis=1, keepdims=True)
    o_win = jnp.dot(pw.astype(BF16), vw, preferred_element_type=F32) / lw

    gates = gates_ref[...]
    for h in range(NSA_HPG):
        r = slice(h * QB, (h + 1) * QB)
        y = (o_cmp[r] * gates[:, 3 * h:3 * h + 1] + o_sel[r] * gates[:, 3 * h + 1:3 * h + 2]
             + o_win[r] * gates[:, 3 * h + 2:3 * h + 3])
        o_ref[:, h * HEAD_DIM:(h + 1) * HEAD_DIM] = y.astype(o_ref.dtype)


def _nsa_attention(proj, kc, vc, gates, imp_m, B, S, QB=256, tk=512):
    G = NSA_GROUPS
    nq = S // QB
    T = B * S
    NC = S // CMP_STRIDE
    gw = NSA_HPG * HEAD_DIM
    rows = NSA_HPG * QB
    kv = lambda base: pl.BlockSpec((S, HEAD_DIM), lambda b, g, q: (b, base + g))
    cblk = pl.BlockSpec((None, None, NC, HEAD_DIM), lambda b, g, q: (b, g, 0, 0))
    onehot = (jnp.arange(S, dtype=I32)[:, None] // SEL_LEN
              == jnp.arange(LANES, dtype=I32)[None, :]).astype(BF16)
    return pl.pallas_call(
        functools.partial(_nsa_attn_kernel, S, QB, tk),
        out_shape=jax.ShapeDtypeStruct((T, NSA_HEADS * HEAD_DIM), BF16),
        grid=(B, G, nq),
        in_specs=[
            pl.BlockSpec((QB, gw), lambda b, g, q: (b * nq + q, g)),
            cblk, cblk,
            kv(4 * G), kv(5 * G), kv(6 * G), kv(7 * G),
            pl.BlockSpec((S, LANES), lambda b, g, q: (0, 0)),
            pl.BlockSpec((QB, LANES), lambda b, g, q: (b * nq + q, g)),
            pl.BlockSpec((NC, LANES), lambda b, g, q: (0, 0)),
        ],
        out_specs=pl.BlockSpec((QB, gw), lambda b, g, q: (b * nq + q, g)),
        scratch_shapes=[pltpu.VMEM((rows, HEAD_DIM), F32), pltpu.VMEM((rows, 2 * HEAD_DIM), BF16)],
        compiler_params=_cparams(("arbitrary", "arbitrary", "arbitrary")),
        name="nsa_attn",
    )(proj, kc, vc, proj, proj, proj, proj, onehot, gates, imp_m)


def _importance_matrix(NC):
    R = SEL_LEN // CMP_STRIDE
    C = CMP_LEN // CMP_STRIDE
    i = jnp.arange(NC, dtype=I32)[:, None]
    j = jnp.arange(LANES, dtype=I32)[None, :]
    w = jnp.zeros((NC, LANES), F32)
    for m in range(R):
        for n in range(C):
            w = w + jnp.where((i == R * j + m - n) & (i < NC - 1), 1.0, 0.0)
    return w.astype(BF16)


def _nsa_mixer(h_bf, cos2, sin2, w_in, ck_pos, ck_w1, ck_w2, cv_pos, cv_w1, cv_w2, B, S):
    D = D_MODEL
    G = NSA_GROUPS
    q_w = w_in[:, :D]
    kvw = w_in[:, D:D + 6 * NSA_KV].reshape(D, N_BRANCH, 2, NSA_KV)
    g_w = w_in[:, D + 6 * NSA_KV:].reshape(D, G, NSA_HPG * N_BRANCH)
    w_main = jnp.concatenate([q_w, kvw[:, 1, 0], kvw[:, 2, 0], kvw[:, 1, 1], kvw[:, 2, 1]], axis=1).astype(BF16)
    w_cmp = jnp.concatenate([kvw[:, 0, 0], kvw[:, 0, 1]], axis=1).astype(BF16)
    w_gate = jnp.pad(g_w, ((0, 0), (0, 0), (0, LANES - NSA_HPG * N_BRANCH))).reshape(D, G * LANES).astype(BF16)

    proj = _proj(h_bf, w_main, cos2, sin2, (D + 2 * NSA_KV) // 512, BF16)
    cmp = _proj(h_bf, w_cmp, cos2, sin2, NSA_KV // 512, F32)
    gates = _gate_proj(h_bf, w_gate)

    NC = S // CMP_STRIDE
    W = CMP_STRIDE * HEAD_DIM

    def chunks(t):
        return t.reshape(B, NC, CMP_STRIDE, G, HEAD_DIM).transpose(0, 3, 1, 2, 4).reshape(B, G, NC, W)

    def halves(w1):
        return w1[:W].astype(BF16), w1[W:].astype(BF16)

    k1t, k1b = halves(ck_w1)
    v1t, v1b = halves(cv_w1)
    kc, vc = _compress(chunks(cmp[:, :NSA_KV]), chunks(cmp[:, NSA_KV:]),
                       ck_pos.reshape(2, W), cv_pos.reshape(2, W),
                       k1t, k1b, ck_w2.astype(BF16), v1t, v1b, cv_w2.astype(BF16))
    return _nsa_attention(proj, kc, vc, gates, _importance_matrix(NC), B, S)


def kernel(x, positions, ln_gain, ln_bias, diff_w_in, diff_lambda_q1, diff_lambda_k1, diff_lambda_q2, diff_lambda_k2, diff_subln_gain, diff_w_o, nsa_w_in, nsa_cmp_k_pos, nsa_cmp_k_w1, nsa_cmp_k_w2, nsa_cmp_v_pos, nsa_cmp_v_w1, nsa_cmp_v_w2, nsa_w_o, moe_router_w, moe_router_b, moe_w_gate_up, moe_b_gate_up, moe_w_down, moe_b_down):
    B, S, D = x.shape
    T = B * S
    assert D == D_MODEL and S % 512 == 0 and S // SEL_LEN <= LANES and S >= WINDOW + NSA_Q_BLOCK

    inv = ROPE_THETA ** (-jnp.arange(0, HEAD_DIM, 2, dtype=F32) / HEAD_DIM)
    ang = positions.astype(F32).reshape(T, 1) * inv[None, :]
    cos2 = jnp.concatenate([jnp.cos(ang), jnp.cos(ang)], axis=1)
    sin2 = jnp.concatenate([-jnp.sin(ang), jnp.sin(ang)], axis=1)

    w_gu = moe_w_gate_up
    w_dn = moe_w_down
    b_gu = moe_b_gate_up.reshape(DEPTH, N_EXPERTS, 1, 2 * D_FF)
    b_dn = moe_b_down.reshape(DEPTH, N_EXPERTS, 1, D)
    rw = jnp.pad(moe_router_w, ((0, 0), (0, 0), (0, LANES - N_EXPERTS)))
    rw_hi = rw.astype(BF16)
    rw_lo = (rw - rw_hi.astype(F32)).astype(BF16)
    rb = jnp.pad(moe_router_b, ((0, 0), (0, LANES - N_EXPERTS))).reshape(DEPTH, 1, LANES)

    h = x.reshape(T, D)
    h_bf = h.astype(BF16)
    for li in range(DEPTH):
        j = li // 2
        if li % 2 == 0:
            lambda_init = 0.8 - 0.6 * math.exp(-0.3 * li)
            proj = _proj(h_bf, diff_w_in[j].astype(BF16), cos2, sin2,
                         4 * DIFF_HEADS * HEAD_DIM // 512, BF16)
            lamv = jnp.stack([diff_lambda_q1[j], diff_lambda_k1[j], diff_lambda_q2[j], diff_lambda_k2[j]])
            mix = _diff_attention(proj, lamv, diff_subln_gain[j].reshape(1, DIFF_V_DIM), lambda_init, B, S)
            w_o = diff_w_o[j].astype(BF16)
        else:
            mix = _nsa_mixer(h_bf, cos2, sin2, nsa_w_in[j], nsa_cmp_k_pos[j], nsa_cmp_k_w1[j], nsa_cmp_k_w2[j],
                             nsa_cmp_v_pos[j], nsa_cmp_v_w1[j], nsa_cmp_v_w2[j], B, S)
            w_o = nsa_w_o[j].astype(BF16)
        h, topi, gate, rank, counts = _out_ln_route(
            mix, w_o, h, ln_gain[li, 0].reshape(1, D), ln_bias[li, 0].reshape(1, D),
            rw_hi[li], rw_lo[li], rb[li])
        h, h_bf = _moe_layer(h, topi, gate, rank, counts, li, w_gu, b_gu, w_dn, b_dn,
                             ln_gain[li, 1].reshape(1, D), ln_bias[li, 1].reshape(1, D))
    return h.reshape(B, S, D)
```

```python
import functools
import math

import jax
import jax.numpy as jnp
from jax import lax
from jax.experimental import pallas as pl
from jax.experimental.pallas import tpu as pltpu

F32 = jnp.float32
BF16 = jnp.bfloat16
I32 = jnp.int32

D_MODEL = 2048
DEPTH = 2
HEAD_DIM = 128
ROPE_THETA = 10000.0

DIFF_HEADS = D_MODEL // (2 * HEAD_DIM)
DIFF_V_DIM = 2 * HEAD_DIM

NSA_HEADS = D_MODEL // HEAD_DIM
NSA_GROUPS = 4
NSA_HPG = NSA_HEADS // NSA_GROUPS
NSA_KV = NSA_GROUPS * HEAD_DIM
N_BRANCH = 3
CMP_LEN = 32
CMP_STRIDE = 16
SEL_LEN = 64
N_SELECT = 16
WINDOW = 512
NSA_Q_BLOCK = 128

N_EXPERTS = 32
TOP_K = 4
D_FF = D_MODEL
SWIGLU_LIMIT = 7.0
SWIGLU_ALPHA = 1.702

DEEPNORM_ALPHA = (2 * DEPTH) ** 0.25
LN_EPS = 1e-5
RMS_EPS = 1e-5
NEG = -1e30
LOG2E = 1.4426950408889634
SM_SCALE_LOG2 = (HEAD_DIM ** -0.5) * LOG2E

LANES = 128
VMEM_LIMIT = 56 * 1024 * 1024
MOE_ROWS = 256


def _cparams(sem):
    return pltpu.CompilerParams(dimension_semantics=sem, vmem_limit_bytes=VMEM_LIMIT)


def _proj_kernel(n_rope_tiles, a_ref, w_ref, cos_ref, sin_ref, o_ref):
    j = pl.program_id(0)
    y = jnp.dot(a_ref[...], w_ref[...], preferred_element_type=F32)
    tn = y.shape[1]

    @pl.when(j < n_rope_tiles)
    def _():
        cos = cos_ref[...]
        sin = sin_ref[...]
        for c in range(tn // HEAD_DIM):
            t = y[:, c * HEAD_DIM:(c + 1) * HEAD_DIM]
            r = t * cos + pltpu.roll(t, HEAD_DIM // 2, 1) * sin
            o_ref[:, c * HEAD_DIM:(c + 1) * HEAD_DIM] = r.astype(o_ref.dtype)

    @pl.when(j >= n_rope_tiles)
    def _():
        o_ref[...] = y.astype(o_ref.dtype)


def _proj(a, w, cos2, sin2, n_rope_tiles, out_dtype, tm=1024, tn=512):
    M, K = a.shape
    N = w.shape[1]
    tm = min(tm, M)
    return pl.pallas_call(
        functools.partial(_proj_kernel, n_rope_tiles),
        out_shape=jax.ShapeDtypeStruct((M, N), out_dtype),
        grid=(N // tn, M // tm),
        in_specs=[
            pl.BlockSpec((tm, K), lambda j, i: (i, 0)),
            pl.BlockSpec((K, tn), lambda j, i: (0, j)),
            pl.BlockSpec((tm, HEAD_DIM), lambda j, i: (i, 0)),
            pl.BlockSpec((tm, HEAD_DIM), lambda j, i: (i, 0)),
        ],
        out_specs=pl.BlockSpec((tm, tn), lambda j, i: (i, j)),
        compiler_params=_cparams(("arbitrary", "arbitrary")),
        name="proj_rope",
    )(a, w, cos2, sin2)


def _gate_kernel(a_ref, w_ref, o_ref):
    y = jnp.dot(a_ref[...], w_ref[...], preferred_element_type=F32)
    o_ref[...] = jax.nn.sigmoid(y)


def _gate_proj(a, w, tm=1024):
    M, K = a.shape
    N = w.shape[1]
    tm = min(tm, M)
    return pl.pallas_call(
        _gate_kernel,
        out_shape=jax.ShapeDtypeStruct((M, N), F32),
        grid=(M // tm,),
        in_specs=[pl.BlockSpec((tm, K), lambda i: (i, 0)),
                  pl.BlockSpec((K, N), lambda i: (0, 0))],
        out_specs=pl.BlockSpec((tm, N), lambda i: (i, 0)),
        compiler_params=_cparams(("arbitrary",)),
        name="nsa_gates",
    )(a, w)


def _diff_attn_kernel(lambda_init, tq, rc, tk, q1_ref, q2_ref, k1_ref, k2_ref, v_ref,
                      lam_ref, g_ref, o_ref, acc1, acc2):
    qi = pl.program_id(2)
    lamv = lam_ref[...]
    lam = (jnp.exp(jnp.sum(lamv[0:1] * lamv[1:2], axis=1, keepdims=True))
           - jnp.exp(jnp.sum(lamv[2:3] * lamv[3:4], axis=1, keepdims=True)) + lambda_init)
    gain = g_ref[...]

    def chunk(r, c0):
        r0 = pl.multiple_of(r * rc, rc)
        s0 = qi * tq + r0
        q1 = q1_ref[pl.ds(r0, rc), :]
        q2 = q2_ref[pl.ds(r0, rc), :]
        acc1[...] = jnp.zeros_like(acc1)
        acc2[...] = jnp.zeros_like(acc2)

        def tile(kt, carry, masked):
            off = pl.multiple_of(kt * tk, tk)
            v = v_ref[pl.ds(off, tk), :]

            def one(q, k_ref, m, l, acc):
                k = k_ref[pl.ds(off, tk), :]
                s = lax.dot_general(q, k, (((1,), (1,)), ((), ())), preferred_element_type=F32)
                if masked:
                    row = s0 + lax.broadcasted_iota(I32, s.shape, 0)
                    col = off + lax.broadcasted_iota(I32, s.shape, 1)
                    s = jnp.where(col <= row, s, NEG)
                m_new = jnp.maximum(m, jnp.max(s, axis=1, keepdims=True))
                alpha = jnp.exp2((m - m_new) * SM_SCALE_LOG2)
                p = jnp.exp2((s - m_new) * SM_SCALE_LOG2)
                l_new = alpha * l + jnp.sum(p, axis=1, keepdims=True)
                acc[...] = alpha * acc[...] + jnp.dot(p.astype(BF16), v, preferred_element_type=F32)
                return m_new, l_new

            m1, l1, m2, l2 = carry
            m1, l1 = one(q1, k1_ref, m1, l1, acc1)
            m2, l2 = one(q2, k2_ref, m2, l2, acc2)
            return m1, l1, m2, l2

        neg = jnp.full((rc, 1), NEG, F32)
        zero = jnp.zeros((rc, 1), F32)
        n_full = s0 // tk
        carry = lax.fori_loop(0, n_full, lambda kt, c: tile(kt, c, False), (neg, zero, neg, zero))
        for j in range(max(1, rc // tk)):
            carry = tile(n_full + j, carry, True)
        _, l1, _, l2 = carry

        a = acc1[...] / l1 - lam * (acc2[...] / l2)
        ms = jnp.mean(a * a, axis=1, keepdims=True)
        o = a * lax.rsqrt(ms + RMS_EPS) * gain
        o_ref[pl.ds(r0, rc), :] = (o * (1.0 - lambda_init)).astype(o_ref.dtype)
        return c0

    lax.fori_loop(0, tq // rc, chunk, 0)


def _diff_attn_t_kernel(lambda_init, tq, tk, q1_ref, q2_ref, k1_ref, k2_ref, vt_ref,
                        lam_ref, g_ref, o_ref, acc1, acc2, s1_buf, s2_buf):
    qi = pl.program_id(2)
    s0 = qi * tq
    nt = (((1,), (1,)), ((), ()))
    lamv = lam_ref[...]
    lam = (jnp.exp(jnp.sum(lamv[0:1] * lamv[1:2], axis=1, keepdims=True))
           - jnp.exp(jnp.sum(lamv[2:3] * lamv[3:4], axis=1, keepdims=True)) + lambda_init)
    acc1[...] = jnp.zeros_like(acc1)
    acc2[...] = jnp.zeros_like(acc2)

    def scores(kt, slot):
        off = pl.multiple_of(kt * tk, tk)
        s1_buf[slot] = lax.dot_general(k1_ref[pl.ds(off, tk), :], q1_ref[...], nt,
                                       preferred_element_type=F32)
        s2_buf[slot] = lax.dot_general(k2_ref[pl.ds(off, tk), :], q2_ref[...], nt,
                                       preferred_element_type=F32)

    def consume(kt, slot, carry, masked):
        off = pl.multiple_of(kt * tk, tk)
        vt = vt_ref[:, pl.ds(off, tk)]

        def one(s_buf, m, l, acc):
            st = s_buf[slot]
            if masked:
                kpos = off + lax.broadcasted_iota(I32, st.shape, 0)
                qpos = s0 + lax.broadcasted_iota(I32, st.shape, 1)
                st = jnp.where(kpos <= qpos, st, NEG)
            m_new = jnp.maximum(m, jnp.max(st, axis=0, keepdims=True))
            alpha = jnp.exp2((m - m_new) * SM_SCALE_LOG2)
            p = jnp.exp2((st - m_new) * SM_SCALE_LOG2)
            l_new = alpha * l + jnp.sum(p, axis=0, keepdims=True)
            acc[...] = alpha * acc[...] + jnp.dot(vt, p.astype(BF16), preferred_element_type=F32)
            return m_new, l_new

        m1, l1, m2, l2 = carry
        m1, l1 = one(s1_buf, m1, l1, acc1)
        m2, l2 = one(s2_buf, m2, l2, acc2)
        return m1, l1, m2, l2

    def pair(tt, carry):
        kt = 2 * tt
        scores(kt + 1, 1)
        carry = consume(kt, 0, carry, False)
        scores(kt + 2, 0)
        return consume(kt + 1, 1, carry, False)

    def odd_tail(_, carry):
        carry = consume(n_full - 1, 0, carry, False)
        scores(n_full, 0)
        return carry

    neg = jnp.full((1, tq), NEG, F32)
    zero = jnp.zeros((1, tq), F32)
    n_full = s0 // tk
    scores(0, 0)
    carry = lax.fori_loop(0, n_full // 2, pair, (neg, zero, neg, zero))
    carry = lax.fori_loop(0, n_full % 2, odd_tail, carry)
    n_edge = max(1, tq // tk)
    for j in range(n_edge):
        if j + 1 < n_edge:
            scores(n_full + j + 1, (j + 1) % 2)
        carry = consume(n_full + j, j % 2, carry, True)
    _, l1, _, l2 = carry

    a = acc1[...] / l1 - lam * (acc2[...] / l2)
    ms = jnp.mean(a * a, axis=0, keepdims=True)
    o = a * lax.rsqrt(ms + RMS_EPS) * g_ref[...]
    o_ref[...] = (o * (1.0 - lambda_init)).T.astype(o_ref.dtype)


def _diff_attention_t(proj, lamv, subln_g, lambda_init, B, S, tq=1024, tk=512):
    H = DIFF_HEADS
    tq = min(tq, S)
    nq = S // tq
    T = B * S
    assert tk % tq == 0 or tq % tk == 0
    v_t = proj[:, 4 * H * HEAD_DIM:].reshape(B, S, H, DIFF_V_DIM).transpose(0, 2, 3, 1)
    kern = functools.partial(_diff_attn_t_kernel, lambda_init, tq, tk)
    return pl.pallas_call(
        kern,
        out_shape=jax.ShapeDtypeStruct((T, H * DIFF_V_DIM), BF16),
        grid=(B, H, nq),
        in_specs=[
            pl.BlockSpec((tq, HEAD_DIM), lambda b, h, q: (b * nq + q, h)),
            pl.BlockSpec((tq, HEAD_DIM), lambda b, h, q: (b * nq + q, H + h)),
            pl.BlockSpec((S, HEAD_DIM), lambda b, h, q: (b, 2 * H + h)),
            pl.BlockSpec((S, HEAD_DIM), lambda b, h, q: (b, 3 * H + h)),
            pl.BlockSpec((None, None, DIFF_V_DIM, S), lambda b, h, q: (b, h, 0, 0)),
            pl.BlockSpec((4, HEAD_DIM), lambda b, h, q: (0, 0)),
            pl.BlockSpec((DIFF_V_DIM, 1), lambda b, h, q: (0, 0)),
        ],
        out_specs=pl.BlockSpec((tq, DIFF_V_DIM), lambda b, h, q: (b * nq + q, h)),
        scratch_shapes=[pltpu.VMEM((DIFF_V_DIM, tq), F32), pltpu.VMEM((DIFF_V_DIM, tq), F32),
                        pltpu.VMEM((2, tk, tq), F32), pltpu.VMEM((2, tk, tq), F32)],
        compiler_params=_cparams(("arbitrary", "arbitrary", "arbitrary")),
        name="diff_attn",
    )(proj, proj, proj, proj, v_t, lamv, subln_g.reshape(DIFF_V_DIM, 1))


def _diff_attention(proj, lamv, subln_g, lambda_init, B, S, tq=1024, rc=1024, tk=512):
    H = DIFF_HEADS
    tq = min(tq, S)
    nq = S // tq
    T = B * S
    assert (tk % rc == 0 or rc % tk == 0) and tq % rc == 0
    kern = functools.partial(_diff_attn_kernel, lambda_init, tq, rc, tk)
    return pl.pallas_call(
        kern,
        out_shape=jax.ShapeDtypeStruct((T, H * DIFF_V_DIM), BF16),
        grid=(B, H, nq),
        in_specs=[
            pl.BlockSpec((tq, HEAD_DIM), lambda b, h, q: (b * nq + q, h)),
            pl.BlockSpec((tq, HEAD_DIM), lambda b, h, q: (b * nq + q, H + h)),
            pl.BlockSpec((S, HEAD_DIM), lambda b, h, q: (b, 2 * H + h)),
            pl.BlockSpec((S, HEAD_DIM), lambda b, h, q: (b, 3 * H + h)),
            pl.BlockSpec((S, DIFF_V_DIM), lambda b, h, q: (b, 2 * H + h)),
            pl.BlockSpec((4, HEAD_DIM), lambda b, h, q: (0, 0)),
            pl.BlockSpec((1, DIFF_V_DIM), lambda b, h, q: (0, 0)),
        ],
        out_specs=pl.BlockSpec((tq, DIFF_V_DIM), lambda b, h, q: (b * nq + q, h)),
        scratch_shapes=[pltpu.VMEM((rc, DIFF_V_DIM), F32), pltpu.VMEM((rc, DIFF_V_DIM), F32)],
        compiler_params=_cparams(("arbitrary", "arbitrary", "arbitrary")),
        name="diff_attn",
    )(proj, proj, proj, proj, proj, lamv, subln_g)


def _layer_norm(z, g, b):
    mu = jnp.mean(z, axis=-1, keepdims=True)
    zc = z - mu
    var = jnp.mean(zc * zc, axis=-1, keepdims=True)
    return zc * lax.rsqrt(var + LN_EPS) * g + b


def _route(hn, rw_hi_ref, rw_lo_ref, rb_ref, topi_ref, gate_ref, rank_ref, counts_ref, run_ref):
    hi = hn.astype(BF16)
    lo = (hn - hi.astype(F32)).astype(BF16)
    w_hi = rw_hi_ref[...]
    logits = (jnp.dot(hi, w_hi, preferred_element_type=F32)
              + jnp.dot(lo, w_hi, preferred_element_type=F32)
              + jnp.dot(hi, rw_lo_ref[...], preferred_element_type=F32)) + rb_ref[...]
    lane = lax.broadcasted_iota(I32, logits.shape, 1).astype(F32)
    cur = jnp.where(lane < N_EXPERTS, logits, -jnp.inf)
    vals, idxs = [], []
    for _ in range(TOP_K):
        m = jnp.max(cur, axis=1, keepdims=True)
        idx = jnp.min(jnp.where(cur == m, lane, float(LANES)), axis=1, keepdims=True)
        vals.append(m)
        idxs.append(idx)
        cur = jnp.where(lane == idx, -jnp.inf, cur)
    es = [jnp.exp(v - vals[0]) for v in vals]
    den = es[0] + es[1] + es[2] + es[3]
    gate_out = jnp.zeros_like(logits)
    idx_out = jnp.zeros_like(logits)
    for k in range(TOP_K):
        gate_out = jnp.where(lane == float(k), es[k] / den, gate_out)
        idx_out = jnp.where(lane == float(k), idxs[k], idx_out)
    gate_ref[...] = gate_out
    topi_ref[...] = idx_out.astype(I32)

    @pl.when(pl.program_id(0) == 0)
    def _():
        run_ref[...] = jnp.zeros_like(run_ref)

    tm = logits.shape[0]
    onehot = jnp.zeros_like(logits)
    for k in range(TOP_K):
        onehot = jnp.where(lane == idxs[k], 1.0, onehot)
    r_i = lax.broadcasted_iota(I32, (tm, tm), 0)
    c_i = lax.broadcasted_iota(I32, (tm, tm), 1)
    tri = jnp.where(r_i > c_i, 1.0, 0.0).astype(BF16)
    pos = jnp.dot(tri, onehot.astype(BF16), preferred_element_type=F32) + run_ref[...]
    rank_out = jnp.zeros_like(logits)
    for k in range(TOP_K):
        rk = jnp.sum(jnp.where(lane == idxs[k], pos, 0.0), axis=1, keepdims=True)
        rank_out = jnp.where(lane == float(k), rk, rank_out)
    rank_ref[...] = rank_out.astype(I32)
    run_ref[...] = run_ref[...] + jnp.sum(onehot, axis=0, keepdims=True)
    counts_ref[...] = run_ref[...].astype(I32)


def _out_ln_kernel(a_ref, w_ref, h_ref, g_ref, b_ref, rw_hi_ref, rw_lo_ref, rb_ref,
                   o_ref, topi_ref, gate_ref, rank_ref, counts_ref, run_ref):
    mix = jnp.dot(a_ref[...], w_ref[...], preferred_element_type=F32)
    hn = _layer_norm(DEEPNORM_ALPHA * h_ref[...] + mix, g_ref[...], b_ref[...])
    o_ref[...] = hn
    _route(hn, rw_hi_ref, rw_lo_ref, rb_ref, topi_ref, gate_ref, rank_ref, counts_ref, run_ref)


def _out_ln_route(a, w_o, h, gain, bias, rw_hi, rw_lo, rb, tm=512):
    T, K = a.shape
    D = w_o.shape[1]
    tm = min(tm, T)
    row = lambda i: (i, 0)
    fixed = lambda i: (0, 0)
    return pl.pallas_call(
        _out_ln_kernel,
        out_shape=(jax.ShapeDtypeStruct((T, D), F32),
                   jax.ShapeDtypeStruct((T, LANES), I32),
                   jax.ShapeDtypeStruct((T, LANES), F32),
                   jax.ShapeDtypeStruct((T, LANES), I32),
                   jax.ShapeDtypeStruct((1, LANES), I32)),
        grid=(T // tm,),
        in_specs=[
            pl.BlockSpec((tm, K), row),
            pl.BlockSpec((K, D), fixed),
            pl.BlockSpec((tm, D), row),
            pl.BlockSpec((1, D), fixed),
            pl.BlockSpec((1, D), fixed),
            pl.BlockSpec((D, LANES), fixed),
            pl.BlockSpec((D, LANES), fixed),
            pl.BlockSpec((1, LANES), fixed),
        ],
        out_specs=(pl.BlockSpec((tm, D), row), pl.BlockSpec((tm, LANES), row),
                   pl.BlockSpec((tm, LANES), row), pl.BlockSpec((tm, LANES), row),
                   pl.BlockSpec((1, LANES), fixed)),
        scratch_shapes=[pltpu.VMEM((1, LANES), F32)],
        compiler_params=_cparams(("arbitrary",)),
        name="out_ln_route",
    )(a, w_o, h, gain, bias, rw_hi, rw_lo, rb)


def _dispatch_kernel(tmd, dest_ref, fill_ref, h_ref, xout_ref, zrow, sem):
    def row_copy(t, d):
        return pltpu.make_async_copy(h_ref.at[pl.ds(t, 1), :], xout_ref.at[pl.ds(d, 1), :], sem)

    def zero_copy(d):
        return pltpu.make_async_copy(zrow.at[pl.ds(0, 1), :], xout_ref.at[pl.ds(d, 1), :], sem)

    def zero_block(b):
        r0 = pl.multiple_of(b * MOE_ROWS, MOE_ROWS)
        return pltpu.make_async_copy(zrow, xout_ref.at[pl.ds(r0, MOE_ROWS), :], sem)

    @pl.when(pl.program_id(0) == 0)
    def _():
        zrow[...] = jnp.zeros_like(zrow)
        n_used = fill_ref[0, N_EXPERTS]
        n_blocks = xout_ref.shape[0] // MOE_ROWS

        def per_expert(e, c):
            lax.fori_loop(fill_ref[0, e], fill_ref[1, e], lambda d, cc: (zero_copy(d).start(), cc)[1], 0)
            return c

        def per_expert_wait(e, c):
            lax.fori_loop(fill_ref[0, e], fill_ref[1, e], lambda d, cc: (zero_copy(d).wait(), cc)[1], 0)
            return c

        lax.fori_loop(0, N_EXPERTS, per_expert, 0)
        lax.fori_loop(n_used, n_blocks, lambda b, cc: (zero_block(b).start(), cc)[1], 0)
        lax.fori_loop(0, N_EXPERTS, per_expert_wait, 0)
        lax.fori_loop(n_used, n_blocks, lambda b, cc: (zero_block(b).wait(), cc)[1], 0)

    def issue(t, c):
        for k in range(TOP_K):
            row_copy(t, dest_ref[t * TOP_K + k]).start(priority=k % 2)
        return c

    def drain(t, c):
        for k in range(TOP_K):
            row_copy(t, dest_ref[t * TOP_K + k]).wait()
        return c

    lax.fori_loop(0, tmd, issue, 0)
    lax.fori_loop(0, tmd, drain, 0)


def _dispatch(h, dest_flat, fill, n_pad, tmd=256):
    T, D = h.shape
    tmd = min(tmd, T)
    return pl.pallas_call(
        functools.partial(_dispatch_kernel, tmd),
        out_shape=jax.ShapeDtypeStruct((n_pad, D), h.dtype),
        grid=(T // tmd,),
        in_specs=[
            pl.BlockSpec((tmd * TOP_K,), lambda i: (i,), memory_space=pltpu.SMEM),
            pl.BlockSpec(memory_space=pltpu.SMEM),
            pl.BlockSpec((tmd, D), lambda i: (i, 0)),
        ],
        out_specs=pl.BlockSpec(memory_space=pl.ANY),
        scratch_shapes=[pltpu.VMEM((MOE_ROWS, D), h.dtype), pltpu.SemaphoreType.DMA(())],
        compiler_params=_cparams(("arbitrary",)),
        name="moe_dispatch",
    )(dest_flat, fill, h)


def _moe_up_kernel(be_ref, nu_ref, x_ref, wg_ref, wu_ref, bg_ref, bu_ref, o_ref):
    i = pl.program_id(1)

    @pl.when(i < nu_ref[0])
    def _():
        x = x_ref[...].astype(BF16)
        g = jnp.dot(x, wg_ref[...].astype(BF16), preferred_element_type=F32) + bg_ref[...]
        u = jnp.dot(x, wu_ref[...].astype(BF16), preferred_element_type=F32) + bu_ref[...]
        g = jnp.minimum(g, SWIGLU_LIMIT)
        u = jnp.clip(u, -SWIGLU_LIMIT, SWIGLU_LIMIT)
        act = g * jax.nn.sigmoid(SWIGLU_ALPHA * g) * (u + 1.0)
        o_ref[...] = act.astype(o_ref.dtype)

    @pl.when(i >= nu_ref[0])
    def _():
        o_ref[...] = jnp.zeros_like(o_ref)


def _moe_up(x_rows, w_gu, b_gu, block_e, n_used, li, tf=1024):
    n_pad, D = x_rows.shape
    R = MOE_ROWS
    nb = n_pad // R
    nf = D_FF // tf

    def xmap(j, i, be, nu):
        return (jnp.minimum(i, nu[0] - 1), 0)

    grid_spec = pltpu.PrefetchScalarGridSpec(
        num_scalar_prefetch=2,
        grid=(nf, nb),
        in_specs=[
            pl.BlockSpec((R, D), xmap),
            pl.BlockSpec((None, None, D, tf), lambda j, i, be, nu: (li, be[i], 0, j)),
            pl.BlockSpec((None, None, D, tf), lambda j, i, be, nu: (li, be[i], 0, nf + j)),
            pl.BlockSpec((None, None, 1, tf), lambda j, i, be, nu: (li, be[i], 0, j)),
            pl.BlockSpec((None, None, 1, tf), lambda j, i, be, nu: (li, be[i], 0, nf + j)),
        ],
        out_specs=pl.BlockSpec((R, tf), lambda j, i, be, nu: (i, j)),
    )
    return pl.pallas_call(
        _moe_up_kernel,
        out_shape=jax.ShapeDtypeStruct((n_pad, D_FF), BF16),
        grid_spec=grid_spec,
        compiler_params=_cparams(("arbitrary", "arbitrary")),
        name="moe_up",
    )(block_e, n_used, x_rows, w_gu, w_gu, b_gu, b_gu)


def _moe_down_kernel(be_ref, nu_ref, a_ref, w_ref, b_ref, o_ref):
    i = pl.program_id(1)

    @pl.when(i < nu_ref[0])
    def _():
        o_ref[...] = jnp.dot(a_ref[...], w_ref[...].astype(BF16), preferred_element_type=F32) + b_ref[...]

    @pl.when(i >= nu_ref[0])
    def _():
        o_ref[...] = jnp.zeros_like(o_ref)


def _moe_down(act, w_down, b_down, block_e, n_used, li, tn=1024):
    n_pad, F = act.shape
    R = MOE_ROWS
    nb = n_pad // R
    D = D_MODEL

    def amap(j, i, be, nu):
        return (jnp.minimum(i, nu[0] - 1), 0)

    grid_spec = pltpu.PrefetchScalarGridSpec(
        num_scalar_prefetch=2,
        grid=(D // tn, nb),
        in_specs=[
            pl.BlockSpec((R, F), amap),
            pl.BlockSpec((None, None, F, tn), lambda j, i, be, nu: (li, be[i], 0, j)),
            pl.BlockSpec((None, None, 1, tn), lambda j, i, be, nu: (li, be[i], 0, j)),
        ],
        out_specs=pl.BlockSpec((R, tn), lambda j, i, be, nu: (i, j)),
    )
    return pl.pallas_call(
        _moe_down_kernel,
        out_shape=jax.ShapeDtypeStruct((n_pad, D), F32),
        grid_spec=grid_spec,
        compiler_params=_cparams(("arbitrary", "arbitrary")),
        name="moe_down",
    )(block_e, n_used, act, w_down, b_down)


def _combine_ln_kernel(tmc, dest_ref, y_ref, gate_ref, h_ref, g_ref, b_ref,
                       o_ref, obf_ref, buf, sem):
    def row_copy(t, k, d):
        return pltpu.make_async_copy(y_ref.at[pl.ds(d, 1), :], buf.at[k, pl.ds(t, 1), :], sem)

    def issue(t, c):
        for k in range(TOP_K):
            row_copy(t, k, dest_ref[t * TOP_K + k]).start(priority=k % 2)
        return c

    def drain(t, c):
        for k in range(TOP_K):
            row_copy(t, k, dest_ref[t * TOP_K + k]).wait()
        return c

    lax.fori_loop(0, tmc, issue, 0)
    lax.fori_loop(0, tmc, drain, 0)

    gate = gate_ref[...]
    ffn = gate[:, 0:1] * buf[0]
    for k in range(1, TOP_K):
        ffn = ffn + gate[:, k:k + 1] * buf[k]
    hn = _layer_norm(DEEPNORM_ALPHA * h_ref[...] + ffn, g_ref[...], b_ref[...])
    o_ref[...] = hn
    obf_ref[...] = hn.astype(BF16)


def _combine_ln(y_rows, dest_flat, gate, h, gain, bias, tmc=256):
    T, D = h.shape
    tmc = min(tmc, T)
    row = lambda i: (i, 0)
    fixed = lambda i: (0, 0)
    return pl.pallas_call(
        functools.partial(_combine_ln_kernel, tmc),
        out_shape=(jax.ShapeDtypeStruct((T, D), F32), jax.ShapeDtypeStruct((T, D), BF16)),
        grid=(T // tmc,),
        in_specs=[
            pl.BlockSpec((tmc * TOP_K,), lambda i: (i,), memory_space=pltpu.SMEM),
            pl.BlockSpec(memory_space=pl.ANY),
            pl.BlockSpec((tmc, LANES), row),
            pl.BlockSpec((tmc, D), row),
            pl.BlockSpec((1, D), fixed),
            pl.BlockSpec((1, D), fixed),
        ],
        out_specs=(pl.BlockSpec((tmc, D), row), pl.BlockSpec((tmc, D), row)),
        scratch_shapes=[pltpu.VMEM((TOP_K, tmc, D), F32), pltpu.SemaphoreType.DMA(())],
        compiler_params=_cparams(("arbitrary",)),
        name="moe_combine_ln",
    )(dest_flat, y_rows, gate, h, gain, bias)


def _moe_layer(h, topi, gate, rank, counts, li, w_gu, b_gu, w_down, b_down, gain, bias):
    T = h.shape[0]
    R = MOE_ROWS
    n_rows = T * TOP_K
    n_blocks = -(-(n_rows + N_EXPERTS * (R - 1)) // R)
    n_pad = n_blocks * R
    top_idx = topi[:, :TOP_K]
    counts = counts[0, :N_EXPERTS]
    padded = ((counts + R - 1) // R) * R
    pad_end = jnp.cumsum(padded)
    pad_start = pad_end - padded
    dest = (pad_start[top_idx] + rank[:, :TOP_K]).astype(I32).reshape(-1)
    block_start = jnp.arange(n_blocks, dtype=I32)[:, None] * R
    block_e = jnp.minimum(jnp.sum((pad_end[None, :] <= block_start).astype(I32), axis=1),
                          N_EXPERTS - 1).astype(I32)
    n_used = (pad_end[-1:] // R).astype(I32)
    fill = jnp.stack([jnp.concatenate([pad_start + counts, n_used]),
                      jnp.concatenate([pad_end, n_used])]).astype(I32)

    x_rows = _dispatch(h, dest, fill, n_pad)
    act = _moe_up(x_rows, w_gu, b_gu, block_e, n_used, li)
    y_rows = _moe_down(act, w_down, b_down, block_e, n_used, li)
    return _combine_ln(y_rows, dest, gate, h, gain, bias)


def _compress_one(c_ref, pos_ref, w1t_ref, w1b_ref, w2_ref, o_ref):
    c = c_ref[...]
    pos = pos_ref[...]
    a = jnp.dot((c + pos[0:1]).astype(BF16), w1t_ref[...], preferred_element_type=F32)
    b = jnp.dot((c + pos[1:2]).astype(BF16), w1b_ref[...], preferred_element_type=F32)
    n = a.shape[0]
    pre = a + pltpu.roll(b, n - 1, 0)
    hid = pre * jax.nn.sigmoid(pre)
    out = jnp.dot(hid.astype(BF16), w2_ref[...], preferred_element_type=F32)
    row = lax.broadcasted_iota(I32, out.shape, 0)
    o_ref[...] = jnp.where(row < n - 1, out, 0.0).astype(o_ref.dtype)


def _compress_kernel(ck_ref, cv_ref, pk_ref, pv_ref, k1t, k1b, k2, v1t, v1b, v2, ok_ref, ov_ref):
    _compress_one(ck_ref, pk_ref, k1t, k1b, k2, ok_ref)
    _compress_one(cv_ref, pv_ref, v1t, v1b, v2, ov_ref)


def _compress(ck, cv, pk, pv, k1t, k1b, k2, v1t, v1b, v2):
    B, G, NC, W = ck.shape
    blk = pl.BlockSpec((None, None, NC, W), lambda b, g: (b, g, 0, 0))
    oblk = pl.BlockSpec((None, None, NC, HEAD_DIM), lambda b, g: (b, g, 0, 0))
    full = lambda shape: pl.BlockSpec(shape, lambda b, g: (0, 0))
    return pl.pallas_call(
        _compress_kernel,
        out_shape=(jax.ShapeDtypeStruct((B, G, NC, HEAD_DIM), BF16),
                   jax.ShapeDtypeStruct((B, G, NC, HEAD_DIM), BF16)),
        grid=(B, G),
        in_specs=[blk, blk, full((2, W)), full((2, W)),
                  full((W, HEAD_DIM)), full((W, HEAD_DIM)), full((HEAD_DIM, HEAD_DIM)),
                  full((W, HEAD_DIM)), full((W, HEAD_DIM)), full((HEAD_DIM, HEAD_DIM))],
        out_specs=(oblk, oblk),
        compiler_params=_cparams(("arbitrary", "arbitrary")),
        name="nsa_compress",
    )(ck, cv, pk, pv, k1t, k1b, k2, v1t, v1b, v2)


def _nsa_attn_kernel(S, QB, tk, q_ref, kc_ref, vc_ref, ks_ref, kw_ref, vst_ref, vw_ref,
                     oh_ref, gates_ref, imp_ref, o_ref, acc, qa, s_buf):
    qi = pl.program_id(2)
    NC = S // CMP_STRIDE
    s0 = qi * QB
    nt = (((1,), (1,)), ((), ()))

    q = q_ref[...]
    qs = jnp.concatenate([q[:, h * HEAD_DIM:(h + 1) * HEAD_DIM] for h in range(NSA_HPG)], axis=0)
    rows = NSA_HPG * QB
    t_q = s0 + lax.broadcasted_iota(I32, (QB, 1), 0)

    def stack(m):
        return jnp.concatenate([m] * NSA_HPG, axis=0)

    sc = lax.dot_general(qs, kc_ref[...], nt, preferred_element_type=F32)
    ncol = lax.broadcasted_iota(I32, (QB, NC), 1)
    cmask = stack(ncol * CMP_STRIDE + (CMP_LEN - 1) <= t_q)
    scm = jnp.where(cmask, sc, NEG)
    mc = jnp.max(scm, axis=1, keepdims=True)
    pc = jnp.where(cmask, jnp.exp2((scm - mc) * SM_SCALE_LOG2), 0.0)
    lc = jnp.sum(pc, axis=1, keepdims=True)
    pc = pc * jnp.where(lc > 0.0, 1.0 / jnp.maximum(lc, 1e-30), 0.0)
    o_cmp = jnp.dot(pc.astype(BF16), vc_ref[...], preferred_element_type=F32)

    pcs = pc[0:QB]
    for h in range(1, NSA_HPG):
        pcs = pcs + pc[h * QB:(h + 1) * QB]
    imp_m = imp_ref[...]
    p_hi = pcs.astype(BF16)
    r1 = pcs - p_hi.astype(F32)
    p_mid = r1.astype(BF16)
    p_lo = (r1 - p_mid.astype(F32)).astype(BF16)
    imp = (jnp.dot(p_hi, imp_m, preferred_element_type=F32)
           + jnp.dot(p_mid, imp_m, preferred_element_type=F32)
           + jnp.dot(p_lo, imp_m, preferred_element_type=F32))

    imp_t = imp.T
    blk_i = lax.broadcasted_iota(I32, (LANES, QB), 0)
    blk = blk_i.astype(F32)
    t_row = s0 + lax.broadcasted_iota(I32, (1, QB), 1)
    cur_blk = t_row // SEL_LEN
    valid = blk_i * SEL_LEN <= t_row
    forced = (blk_i == 0) | (blk_i == cur_blk) | (blk_i == cur_blk - 1)
    score = jnp.where(valid, jnp.where(forced, jnp.inf, imp_t), -1.0)
    sel_t = jnp.zeros((LANES, QB), F32)
    for _ in range(N_SELECT):
        m = jnp.max(score, axis=0, keepdims=True)
        idx = jnp.min(jnp.where(score == m, blk, float(LANES)), axis=0, keepdims=True)
        hit = blk == idx
        sel_t = jnp.where(hit, 1.0, sel_t)
        score = jnp.where(hit, -2.0, score)
    sel_bias = ((sel_t - 1.0) * (-NEG)).T.astype(BF16)
    qa[:, 0:HEAD_DIM] = qs
    qa[:, HEAD_DIM:2 * HEAD_DIM] = stack(sel_bias)

    acc[...] = jnp.zeros_like(acc)

    def sel_scores(kt, slot):
        off = pl.multiple_of(kt * tk, tk)
        k = jnp.concatenate([ks_ref[pl.ds(off, tk), :], oh_ref[pl.ds(off, tk), :]], axis=1)
        s_buf[slot] = lax.dot_general(k, qa[...], nt, preferred_element_type=F32)

    def sel_tile(kt, slot, carry, causal):
        m, l = carry
        off = pl.multiple_of(kt * tk, tk)
        vt = vst_ref[:, pl.ds(off, tk)]
        s = s_buf[slot]
        if causal:
            kpos = off + lax.broadcasted_iota(I32, (tk, QB), 0)
            qpos = s0 + lax.broadcasted_iota(I32, (tk, QB), 1)
            bias = jnp.where(kpos <= qpos, 0.0, NEG)
            s = s + jnp.concatenate([bias] * NSA_HPG, axis=1)
        m_new = jnp.maximum(m, jnp.max(s, axis=0, keepdims=True))
        alpha = jnp.exp2((m - m_new) * SM_SCALE_LOG2)
        p = jnp.exp2((s - m_new) * SM_SCALE_LOG2)
        l_new = alpha * l + jnp.sum(p, axis=0, keepdims=True)
        acc[...] = alpha * acc[...] + jnp.dot(vt, p.astype(BF16), preferred_element_type=F32)
        return m_new, l_new

    def sel_pair(tt, carry):
        kt = 2 * tt
        sel_scores(kt + 1, 1)
        carry = sel_tile(kt, 0, carry, False)
        sel_scores(kt + 2, 0)
        return sel_tile(kt + 1, 1, carry, False)

    def sel_odd(_, carry):
        carry = sel_tile(n_full - 1, 0, carry, False)
        sel_scores(n_full, 0)
        return carry

    n_full = s0 // tk
    sel_scores(0, 0)
    carry = lax.fori_loop(0, n_full // 2, sel_pair,
                          (jnp.full((1, rows), NEG, F32), jnp.zeros((1, rows), F32)))
    carry = lax.fori_loop(0, n_full % 2, sel_odd, carry)
    _, l_sel = sel_tile(n_full, 0, carry, True)
    o_sel = (acc[...] / l_sel).T

    WK = WINDOW + QB
    start = pl.multiple_of(jnp.maximum(s0 - WINDOW, 0), NSA_Q_BLOCK)
    kw = kw_ref[pl.ds(start, WK), :]
    vw = vw_ref[pl.ds(start, WK), :]
    sw = lax.dot_general(qs, kw, nt, preferred_element_type=F32)
    wpos = start + lax.broadcasted_iota(I32, (QB, WK), 1)
    wok = (wpos <= t_q) & (wpos > t_q - WINDOW)
    sw = sw + stack(jnp.where(wok, 0.0, NEG))
    mw = jnp.max(sw, axis=1, keepdims=True)
    pw = jnp.exp2((sw - mw) * SM_SCALE_LOG2)
    lw = jnp.sum(pw, axis=1, keepdims=True)
    o_win = jnp.dot(pw.astype(BF16), vw, preferred_element_type=F32) / lw

    gates = gates_ref[...]
    for h in range(NSA_HPG):
        r = slice(h * QB, (h + 1) * QB)
        y = (o_cmp[r] * gates[:, 3 * h:3 * h + 1] + o_sel[r] * gates[:, 3 * h + 1:3 * h + 2]
             + o_win[r] * gates[:, 3 * h + 2:3 * h + 3])
        o_ref[:, h * HEAD_DIM:(h + 1) * HEAD_DIM] = y.astype(o_ref.dtype)


def _nsa_attention(proj, kc, vc, gates, imp_m, B, S, QB=256, tk=512):
    G = NSA_GROUPS
    nq = S // QB
    T = B * S
    NC = S // CMP_STRIDE
    gw = NSA_HPG * HEAD_DIM
    rows = NSA_HPG * QB
    kv = lambda base: pl.BlockSpec((S, HEAD_DIM), lambda b, g, q: (b, base + g))
    cblk = pl.BlockSpec((None, None, NC, HEAD_DIM), lambda b, g, q: (b, g, 0, 0))
    onehot = (jnp.arange(S, dtype=I32)[:, None] // SEL_LEN
              == jnp.arange(LANES, dtype=I32)[None, :]).astype(BF16)
    vs_t = proj[:, 6 * G * HEAD_DIM:7 * G * HEAD_DIM].reshape(B, S, G, HEAD_DIM).transpose(0, 2, 3, 1)
    return pl.pallas_call(
        functools.partial(_nsa_attn_kernel, S, QB, tk),
        out_shape=jax.ShapeDtypeStruct((T, NSA_HEADS * HEAD_DIM), BF16),
        grid=(B, G, nq),
        in_specs=[
            pl.BlockSpec((QB, gw), lambda b, g, q: (b * nq + q, g)),
            cblk, cblk,
            kv(4 * G), kv(5 * G),
            pl.BlockSpec((None, None, HEAD_DIM, S), lambda b, g, q: (b, g, 0, 0)),
            kv(7 * G),
            pl.BlockSpec((S, LANES), lambda b, g, q: (0, 0)),
            pl.BlockSpec((QB, LANES), lambda b, g, q: (b * nq + q, g)),
            pl.BlockSpec((NC, LANES), lambda b, g, q: (0, 0)),
        ],
        out_specs=pl.BlockSpec((QB, gw), lambda b, g, q: (b * nq + q, g)),
        scratch_shapes=[pltpu.VMEM((HEAD_DIM, rows), F32), pltpu.VMEM((rows, 2 * HEAD_DIM), BF16),
                        pltpu.VMEM((2, tk, rows), F32)],
        compiler_params=_cparams(("arbitrary", "arbitrary", "arbitrary")),
        name="nsa_attn",
    )(proj, kc, vc, proj, proj, vs_t, proj, onehot, gates, imp_m)


def _importance_matrix(NC):
    R = SEL_LEN // CMP_STRIDE
    C = CMP_LEN // CMP_STRIDE
    i = jnp.arange(NC, dtype=I32)[:, None]
    j = jnp.arange(LANES, dtype=I32)[None, :]
    w = jnp.zeros((NC, LANES), F32)
    for m in range(R):
        for n in range(C):
            w = w + jnp.where((i == R * j + m - n) & (i < NC - 1), 1.0, 0.0)
    return w.astype(BF16)


def _nsa_mixer(h_bf, cos2, sin2, w_in, ck_pos, ck_w1, ck_w2, cv_pos, cv_w1, cv_w2, B, S):
    D = D_MODEL
    G = NSA_GROUPS
    q_w = w_in[:, :D]
    kvw = w_in[:, D:D + 6 * NSA_KV].reshape(D, N_BRANCH, 2, NSA_KV)
    g_w = w_in[:, D + 6 * NSA_KV:].reshape(D, G, NSA_HPG * N_BRANCH)
    w_main = jnp.concatenate([q_w, kvw[:, 1, 0], kvw[:, 2, 0], kvw[:, 1, 1], kvw[:, 2, 1]], axis=1).astype(BF16)
    w_cmp = jnp.concatenate([kvw[:, 0, 0], kvw[:, 0, 1]], axis=1).astype(BF16)
    w_gate = jnp.pad(g_w, ((0, 0), (0, 0), (0, LANES - NSA_HPG * N_BRANCH))).reshape(D, G * LANES).astype(BF16)

    proj = _proj(h_bf, w_main, cos2, sin2, (D + 2 * NSA_KV) // 512, BF16)
    cmp = _proj(h_bf, w_cmp, cos2, sin2, NSA_KV // 512, F32)
    gates = _gate_proj(h_bf, w_gate)

    NC = S // CMP_STRIDE
    W = CMP_STRIDE * HEAD_DIM

    def chunks(t):
        return t.reshape(B, NC, CMP_STRIDE, G, HEAD_DIM).transpose(0, 3, 1, 2, 4).reshape(B, G, NC, W)

    def halves(w1):
        return w1[:W].astype(BF16), w1[W:].astype(BF16)

    k1t, k1b = halves(ck_w1)
    v1t, v1b = halves(cv_w1)
    kc, vc = _compress(chunks(cmp[:, :NSA_KV]), chunks(cmp[:, NSA_KV:]),
                       ck_pos.reshape(2, W), cv_pos.reshape(2, W),
                       k1t, k1b, ck_w2.astype(BF16), v1t, v1b, cv_w2.astype(BF16))
    return _nsa_attention(proj, kc, vc, gates, _importance_matrix(NC), B, S)


def kernel(x, positions, ln_gain, ln_bias, diff_w_in, diff_lambda_q1, diff_lambda_k1, diff_lambda_q2, diff_lambda_k2, diff_subln_gain, diff_w_o, nsa_w_in, nsa_cmp_k_pos, nsa_cmp_k_w1, nsa_cmp_k_w2, nsa_cmp_v_pos, nsa_cmp_v_w1, nsa_cmp_v_w2, nsa_w_o, moe_router_w, moe_router_b, moe_w_gate_up, moe_b_gate_up, moe_w_down, moe_b_down):
    B, S, D = x.shape
    T = B * S
    assert D == D_MODEL and S % 512 == 0 and S // SEL_LEN <= LANES and S >= WINDOW + NSA_Q_BLOCK

    inv = ROPE_THETA ** (-jnp.arange(0, HEAD_DIM, 2, dtype=F32) / HEAD_DIM)
    ang = positions.astype(F32).reshape(T, 1) * inv[None, :]
    cos2 = jnp.concatenate([jnp.cos(ang), jnp.cos(ang)], axis=1)
    sin2 = jnp.concatenate([-jnp.sin(ang), jnp.sin(ang)], axis=1)

    w_gu = moe_w_gate_up
    w_dn = moe_w_down
    b_gu = moe_b_gate_up.reshape(DEPTH, N_EXPERTS, 1, 2 * D_FF)
    b_dn = moe_b_down.reshape(DEPTH, N_EXPERTS, 1, D)
    rw = jnp.pad(moe_router_w, ((0, 0), (0, 0), (0, LANES - N_EXPERTS)))
    rw_hi = rw.astype(BF16)
    rw_lo = (rw - rw_hi.astype(F32)).astype(BF16)
    rb = jnp.pad(moe_router_b, ((0, 0), (0, LANES - N_EXPERTS))).reshape(DEPTH, 1, LANES)

    h = x.reshape(T, D)
    h_bf = h.astype(BF16)
    for li in range(DEPTH):
        j = li // 2
        if li % 2 == 0:
            lambda_init = 0.8 - 0.6 * math.exp(-0.3 * li)
            proj = _proj(h_bf, diff_w_in[j].astype(BF16), cos2, sin2,
                         4 * DIFF_HEADS * HEAD_DIM // 512, BF16)
            lamv = jnp.stack([diff_lambda_q1[j], diff_lambda_k1[j], diff_lambda_q2[j], diff_lambda_k2[j]])
            mix = _diff_attention_t(proj, lamv, diff_subln_gain[j], lambda_init, B, S)
            w_o = diff_w_o[j].astype(BF16)
        else:
            mix = _nsa_mixer(h_bf, cos2, sin2, nsa_w_in[j], nsa_cmp_k_pos[j], nsa_cmp_k_w1[j], nsa_cmp_k_w2[j],
                             nsa_cmp_v_pos[j], nsa_cmp_v_w1[j], nsa_cmp_v_w2[j], B, S)
            w_o = nsa_w_o[j].astype(BF16)
        h, topi, gate, rank, counts = _out_ln_route(
            mix, w_o, h, ln_gain[li, 0].reshape(1, D), ln_bias[li, 0].reshape(1, D),
            rw_hi[li], rw_lo[li], rb[li])
        h, h_bf = _moe_layer(h, topi, gate, rank, counts, li, w_gu, b_gu, w_dn, b_dn,
                             ln_gain[li, 1].reshape(1, D), ln_bias[li, 1].reshape(1, D))
    return h.reshape(B, S, D)
```

```python
import functools
import math

import jax
import jax.numpy as jnp
from jax import lax
from jax.experimental import pallas as pl
from jax.experimental.pallas import tpu as pltpu

F32 = jnp.float32
BF16 = jnp.bfloat16
I32 = jnp.int32

D_MODEL = 2048
DEPTH = 2
HEAD_DIM = 128
ROPE_THETA = 10000.0

DIFF_HEADS = D_MODEL // (2 * HEAD_DIM)
DIFF_V_DIM = 2 * HEAD_DIM

NSA_HEADS = D_MODEL // HEAD_DIM
NSA_GROUPS = 4
NSA_HPG = NSA_HEADS // NSA_GROUPS
NSA_KV = NSA_GROUPS * HEAD_DIM
N_BRANCH = 3
CMP_LEN = 32
CMP_STRIDE = 16
SEL_LEN = 64
N_SELECT = 16
WINDOW = 512
NSA_Q_BLOCK = 128

N_EXPERTS = 32
TOP_K = 4
D_FF = D_MODEL
SWIGLU_LIMIT = 7.0
SWIGLU_ALPHA = 1.702

DEEPNORM_ALPHA = (2 * DEPTH) ** 0.25
LN_EPS = 1e-5
RMS_EPS = 1e-5
NEG = -1e30
LOG2E = 1.4426950408889634
SM_SCALE_LOG2 = (HEAD_DIM ** -0.5) * LOG2E

LANES = 128
VMEM_LIMIT = 56 * 1024 * 1024
MOE_ROWS = 512


def _cparams(sem):
    return pltpu.CompilerParams(dimension_semantics=sem, vmem_limit_bytes=VMEM_LIMIT)


def _proj_kernel(n_rope_tiles, a_ref, w_ref, cos_ref, sin_ref, o_ref):
    j = pl.program_id(0)
    y = jnp.dot(a_ref[...], w_ref[...], preferred_element_type=F32)
    tn = y.shape[1]

    @pl.when(j < n_rope_tiles)
    def _():
        cos = cos_ref[...]
        sin = sin_ref[...]
        for c in range(tn // HEAD_DIM):
            t = y[:, c * HEAD_DIM:(c + 1) * HEAD_DIM]
            r = t * cos + pltpu.roll(t, HEAD_DIM // 2, 1) * sin
            o_ref[:, c * HEAD_DIM:(c + 1) * HEAD_DIM] = r.astype(o_ref.dtype)

    @pl.when(j >= n_rope_tiles)
    def _():
        o_ref[...] = y.astype(o_ref.dtype)


def _proj(a, w, cos2, sin2, n_rope_tiles, out_dtype, tm=1024, tn=512):
    M, K = a.shape
    N = w.shape[1]
    tm = min(tm, M)
    return pl.pallas_call(
        functools.partial(_proj_kernel, n_rope_tiles),
        out_shape=jax.ShapeDtypeStruct((M, N), out_dtype),
        grid=(N // tn, M // tm),
        in_specs=[
            pl.BlockSpec((tm, K), lambda j, i: (i, 0)),
            pl.BlockSpec((K, tn), lambda j, i: (0, j)),
            pl.BlockSpec((tm, HEAD_DIM), lambda j, i: (i, 0)),
            pl.BlockSpec((tm, HEAD_DIM), lambda j, i: (i, 0)),
        ],
        out_specs=pl.BlockSpec((tm, tn), lambda j, i: (i, j)),
        compiler_params=_cparams(("arbitrary", "arbitrary")),
        name="proj_rope",
    )(a, w, cos2, sin2)


def _gate_kernel(a_ref, w_ref, o_ref):
    y = jnp.dot(a_ref[...], w_ref[...], preferred_element_type=F32)
    o_ref[...] = jax.nn.sigmoid(y)


def _gate_proj(a, w, tm=1024):
    M, K = a.shape
    N = w.shape[1]
    tm = min(tm, M)
    return pl.pallas_call(
        _gate_kernel,
        out_shape=jax.ShapeDtypeStruct((M, N), F32),
        grid=(M // tm,),
        in_specs=[pl.BlockSpec((tm, K), lambda i: (i, 0)),
                  pl.BlockSpec((K, N), lambda i: (0, 0))],
        out_specs=pl.BlockSpec((tm, N), lambda i: (i, 0)),
        compiler_params=_cparams(("arbitrary",)),
        name="nsa_gates",
    )(a, w)


def _diff_attn_kernel(lambda_init, tq, tk, q1_ref, q2_ref, k1_ref, k2_ref, vt_ref,
                      lam_ref, g_ref, o_ref, acc1, acc2, s1_buf, s2_buf):
    qi = pl.program_id(2)
    s0 = qi * tq
    nt = (((1,), (1,)), ((), ()))
    lamv = lam_ref[...]
    lam = (jnp.exp(jnp.sum(lamv[0:1] * lamv[1:2], axis=1, keepdims=True))
           - jnp.exp(jnp.sum(lamv[2:3] * lamv[3:4], axis=1, keepdims=True)) + lambda_init)
    acc1[...] = jnp.zeros_like(acc1)
    acc2[...] = jnp.zeros_like(acc2)

    def scores(kt, slot):
        off = pl.multiple_of(kt * tk, tk)
        s1_buf[slot] = lax.dot_general(k1_ref[pl.ds(off, tk), :], q1_ref[...], nt,
                                       preferred_element_type=F32)
        s2_buf[slot] = lax.dot_general(k2_ref[pl.ds(off, tk), :], q2_ref[...], nt,
                                       preferred_element_type=F32)

    def consume(kt, slot, carry, masked):
        off = pl.multiple_of(kt * tk, tk)
        vt = vt_ref[:, pl.ds(off, tk)]

        def one(s_buf, m, l, acc):
            st = s_buf[slot]
            if masked:
                kpos = off + lax.broadcasted_iota(I32, st.shape, 0)
                qpos = s0 + lax.broadcasted_iota(I32, st.shape, 1)
                st = jnp.where(kpos <= qpos, st, NEG)
            m_new = jnp.maximum(m, jnp.max(st, axis=0, keepdims=True))
            alpha = jnp.exp2((m - m_new) * SM_SCALE_LOG2)
            p = jnp.exp2((st - m_new) * SM_SCALE_LOG2)
            l_new = alpha * l + jnp.sum(p, axis=0, keepdims=True)
            acc[...] = alpha * acc[...] + jnp.dot(vt, p.astype(BF16), preferred_element_type=F32)
            return m_new, l_new

        m1, l1, m2, l2 = carry
        m1, l1 = one(s1_buf, m1, l1, acc1)
        m2, l2 = one(s2_buf, m2, l2, acc2)
        return m1, l1, m2, l2

    def pair(tt, carry):
        kt = 2 * tt
        scores(kt + 1, 1)
        carry = consume(kt, 0, carry, False)
        scores(kt + 2, 0)
        return consume(kt + 1, 1, carry, False)

    def odd_tail(_, carry):
        carry = consume(n_full - 1, 0, carry, False)
        scores(n_full, 0)
        return carry

    neg = jnp.full((1, tq), NEG, F32)
    zero = jnp.zeros((1, tq), F32)
    n_full = s0 // tk
    scores(0, 0)
    carry = lax.fori_loop(0, n_full // 2, pair, (neg, zero, neg, zero))
    carry = lax.fori_loop(0, n_full % 2, odd_tail, carry)
    n_edge = max(1, tq // tk)
    for j in range(n_edge):
        if j + 1 < n_edge:
            scores(n_full + j + 1, (j + 1) % 2)
        carry = consume(n_full + j, j % 2, carry, True)
    _, l1, _, l2 = carry

    a = acc1[...] / l1 - lam * (acc2[...] / l2)
    ms = jnp.mean(a * a, axis=0, keepdims=True)
    o = a * lax.rsqrt(ms + RMS_EPS) * g_ref[...]
    o_ref[...] = (o * (1.0 - lambda_init)).T.astype(o_ref.dtype)


def _diff_attention(proj, lamv, subln_g, lambda_init, B, S, tq=1024, tk=1024):
    H = DIFF_HEADS
    tq = min(tq, S)
    nq = S // tq
    T = B * S
    assert tk % tq == 0 or tq % tk == 0
    v_t = proj[:, 4 * H * HEAD_DIM:].reshape(B, S, H, DIFF_V_DIM).transpose(0, 2, 3, 1)
    kern = functools.partial(_diff_attn_kernel, lambda_init, tq, tk)
    return pl.pallas_call(
        kern,
        out_shape=jax.ShapeDtypeStruct((T, H * DIFF_V_DIM), BF16),
        grid=(B, H, nq),
        in_specs=[
            pl.BlockSpec((tq, HEAD_DIM), lambda b, h, q: (b * nq + q, h)),
            pl.BlockSpec((tq, HEAD_DIM), lambda b, h, q: (b * nq + q, H + h)),
            pl.BlockSpec((S, HEAD_DIM), lambda b, h, q: (b, 2 * H + h)),
            pl.BlockSpec((S, HEAD_DIM), lambda b, h, q: (b, 3 * H + h)),
            pl.BlockSpec((None, None, DIFF_V_DIM, S), lambda b, h, q: (b, h, 0, 0)),
            pl.BlockSpec((4, HEAD_DIM), lambda b, h, q: (0, 0)),
            pl.BlockSpec((DIFF_V_DIM, 1), lambda b, h, q: (0, 0)),
        ],
        out_specs=pl.BlockSpec((tq, DIFF_V_DIM), lambda b, h, q: (b * nq + q, h)),
        scratch_shapes=[pltpu.VMEM((DIFF_V_DIM, tq), F32), pltpu.VMEM((DIFF_V_DIM, tq), F32),
                        pltpu.VMEM((2, tk, tq), F32), pltpu.VMEM((2, tk, tq), F32)],
        compiler_params=_cparams(("arbitrary", "arbitrary", "arbitrary")),
        name="diff_attn",
    )(proj, proj, proj, proj, v_t, lamv, subln_g.reshape(DIFF_V_DIM, 1))


def _layer_norm(z, g, b):
    mu = jnp.mean(z, axis=-1, keepdims=True)
    zc = z - mu
    var = jnp.mean(zc * zc, axis=-1, keepdims=True)
    return zc * lax.rsqrt(var + LN_EPS) * g + b


def _route(hn, rw_hi_ref, rw_lo_ref, rb_ref, topi_ref, gate_ref, rank_ref, counts_ref, run_ref):
    hi = hn.astype(BF16)
    lo = (hn - hi.astype(F32)).astype(BF16)
    w_hi = rw_hi_ref[...]
    logits = (jnp.dot(hi, w_hi, preferred_element_type=F32)
              + jnp.dot(lo, w_hi, preferred_element_type=F32)
              + jnp.dot(hi, rw_lo_ref[...], preferred_element_type=F32)) + rb_ref[...]
    lane = lax.broadcasted_iota(I32, logits.shape, 1).astype(F32)
    cur = jnp.where(lane < N_EXPERTS, logits, -jnp.inf)
    vals, idxs = [], []
    for _ in range(TOP_K):
        m = jnp.max(cur, axis=1, keepdims=True)
        idx = jnp.min(jnp.where(cur == m, lane, float(LANES)), axis=1, keepdims=True)
        vals.append(m)
        idxs.append(idx)
        cur = jnp.where(lane == idx, -jnp.inf, cur)
    es = [jnp.exp(v - vals[0]) for v in vals]
    den = es[0] + es[1] + es[2] + es[3]
    gate_out = jnp.zeros_like(logits)
    idx_out = jnp.zeros_like(logits)
    for k in range(TOP_K):
        gate_out = jnp.where(lane == float(k), es[k] / den, gate_out)
        idx_out = jnp.where(lane == float(k), idxs[k], idx_out)
    gate_ref[...] = gate_out
    topi_ref[...] = idx_out.astype(I32)

    @pl.when(pl.program_id(0) == 0)
    def _():
        run_ref[...] = jnp.zeros_like(run_ref)

    tm = logits.shape[0]
    onehot = jnp.zeros_like(logits)
    for k in range(TOP_K):
        onehot = jnp.where(lane == idxs[k], 1.0, onehot)
    r_i = lax.broadcasted_iota(I32, (tm, tm), 0)
    c_i = lax.broadcasted_iota(I32, (tm, tm), 1)
    tri = jnp.where(r_i > c_i, 1.0, 0.0).astype(BF16)
    pos = jnp.dot(tri, onehot.astype(BF16), preferred_element_type=F32) + run_ref[...]
    rank_out = jnp.zeros_like(logits)
    for k in range(TOP_K):
        rk = jnp.sum(jnp.where(lane == idxs[k], pos, 0.0), axis=1, keepdims=True)
        rank_out = jnp.where(lane == float(k), rk, rank_out)
    rank_ref[...] = rank_out.astype(I32)
    run_ref[...] = run_ref[...] + jnp.sum(onehot, axis=0, keepdims=True)
    counts_ref[...] = run_ref[...].astype(I32)


def _out_ln_kernel(a_ref, w_ref, h_ref, g_ref, b_ref, rw_hi_ref, rw_lo_ref, rb_ref,
                   o_ref, topi_ref, gate_ref, rank_ref, counts_ref, run_ref):
    mix = jnp.dot(a_ref[...], w_ref[...], preferred_element_type=F32)
    hn = _layer_norm(DEEPNORM_ALPHA * h_ref[...] + mix, g_ref[...], b_ref[...])
    o_ref[...] = hn
    _route(hn, rw_hi_ref, rw_lo_ref, rb_ref, topi_ref, gate_ref, rank_ref, counts_ref, run_ref)


def _out_ln_route(a, w_o, h, gain, bias, rw_hi, rw_lo, rb, tm=512):
    T, K = a.shape
    D = w_o.shape[1]
    tm = min(tm, T)
    row = lambda i: (i, 0)
    fixed = lambda i: (0, 0)
    return pl.pallas_call(
        _out_ln_kernel,
        out_shape=(jax.ShapeDtypeStruct((T, D), F32),
                   jax.ShapeDtypeStruct((T, LANES), I32),
                   jax.ShapeDtypeStruct((T, LANES), F32),
                   jax.ShapeDtypeStruct((T, LANES), I32),
                   jax.ShapeDtypeStruct((1, LANES), I32)),
        grid=(T // tm,),
        in_specs=[
            pl.BlockSpec((tm, K), row),
            pl.BlockSpec((K, D), fixed),
            pl.BlockSpec((tm, D), row),
            pl.BlockSpec((1, D), fixed),
            pl.BlockSpec((1, D), fixed),
            pl.BlockSpec((D, LANES), fixed),
            pl.BlockSpec((D, LANES), fixed),
            pl.BlockSpec((1, LANES), fixed),
        ],
        out_specs=(pl.BlockSpec((tm, D), row), pl.BlockSpec((tm, LANES), row),
                   pl.BlockSpec((tm, LANES), row), pl.BlockSpec((tm, LANES), row),
                   pl.BlockSpec((1, LANES), fixed)),
        scratch_shapes=[pltpu.VMEM((1, LANES), F32)],
        compiler_params=_cparams(("arbitrary",)),
        name="out_ln_route",
    )(a, w_o, h, gain, bias, rw_hi, rw_lo, rb)


def _dispatch_kernel(tmd, dest_ref, fill_ref, h_ref, xout_ref, zrow, sem):
    def row_copy(t, d):
        return pltpu.make_async_copy(h_ref.at[pl.ds(t, 1), :], xout_ref.at[pl.ds(d, 1), :], sem)

    def zero_copy(d):
        return pltpu.make_async_copy(zrow.at[pl.ds(0, 1), :], xout_ref.at[pl.ds(d, 1), :], sem)

    def zero_block(b):
        r0 = pl.multiple_of(b * MOE_ROWS, MOE_ROWS)
        return pltpu.make_async_copy(zrow, xout_ref.at[pl.ds(r0, MOE_ROWS), :], sem)

    @pl.when(pl.program_id(0) == 0)
    def _():
        zrow[...] = jnp.zeros_like(zrow)
        n_used = fill_ref[0, N_EXPERTS]
        n_blocks = xout_ref.shape[0] // MOE_ROWS

        def per_expert(e, c):
            lax.fori_loop(fill_ref[0, e], fill_ref[1, e], lambda d, cc: (zero_copy(d).start(), cc)[1], 0)
            return c

        def per_expert_wait(e, c):
            lax.fori_loop(fill_ref[0, e], fill_ref[1, e], lambda d, cc: (zero_copy(d).wait(), cc)[1], 0)
            return c

        lax.fori_loop(0, N_EXPERTS, per_expert, 0)
        lax.fori_loop(n_used, n_blocks, lambda b, cc: (zero_block(b).start(), cc)[1], 0)
        lax.fori_loop(0, N_EXPERTS, per_expert_wait, 0)
        lax.fori_loop(n_used, n_blocks, lambda b, cc: (zero_block(b).wait(), cc)[1], 0)

    def issue(t, c):
        for k in range(TOP_K):
            row_copy(t, dest_ref[t * TOP_K + k]).start()
        return c

    def drain(t, c):
        for k in range(TOP_K):
            row_copy(t, dest_ref[t * TOP_K + k]).wait()
        return c

    lax.fori_loop(0, tmd, issue, 0)
    lax.fori_loop(0, tmd, drain, 0)


def _dispatch(h, dest_flat, fill, n_pad, tmd=256):
    T, D = h.shape
    tmd = min(tmd, T)
    return pl.pallas_call(
        functools.partial(_dispatch_kernel, tmd),
        out_shape=jax.ShapeDtypeStruct((n_pad, D), h.dtype),
        grid=(T // tmd,),
        in_specs=[
            pl.BlockSpec((tmd * TOP_K,), lambda i: (i,), memory_space=pltpu.SMEM),
            pl.BlockSpec(memory_space=pltpu.SMEM),
            pl.BlockSpec((tmd, D), lambda i: (i, 0)),
        ],
        out_specs=pl.BlockSpec(memory_space=pl.ANY),
        scratch_shapes=[pltpu.VMEM((MOE_ROWS, D), h.dtype), pltpu.SemaphoreType.DMA(())],
        compiler_params=_cparams(("arbitrary",)),
        name="moe_dispatch",
    )(dest_flat, fill, h)


def _moe_up_kernel(be_ref, nu_ref, x_ref, wg_ref, wu_ref, bg_ref, bu_ref, o_ref):
    i = pl.program_id(1)

    @pl.when(i < nu_ref[0])
    def _():
        x = x_ref[...].astype(BF16)
        g = jnp.dot(x, wg_ref[...].astype(BF16), preferred_element_type=F32) + bg_ref[...]
        u = jnp.dot(x, wu_ref[...].astype(BF16), preferred_element_type=F32) + bu_ref[...]
        g = jnp.minimum(g, SWIGLU_LIMIT)
        u = jnp.clip(u, -SWIGLU_LIMIT, SWIGLU_LIMIT)
        act = g * jax.nn.sigmoid(SWIGLU_ALPHA * g) * (u + 1.0)
        o_ref[...] = act.astype(o_ref.dtype)

    @pl.when(i >= nu_ref[0])
    def _():
        o_ref[...] = jnp.zeros_like(o_ref)


def _moe_up(x_rows, w_gu, b_gu, block_e, n_used, li, tf=1024):
    n_pad, D = x_rows.shape
    R = MOE_ROWS
    nb = n_pad // R
    nf = D_FF // tf

    def xmap(j, i, be, nu):
        return (jnp.minimum(i, nu[0] - 1), 0)

    grid_spec = pltpu.PrefetchScalarGridSpec(
        num_scalar_prefetch=2,
        grid=(nf, nb),
        in_specs=[
            pl.BlockSpec((R, D), xmap),
            pl.BlockSpec((None, None, D, tf), lambda j, i, be, nu: (li, be[i], 0, j)),
            pl.BlockSpec((None, None, D, tf), lambda j, i, be, nu: (li, be[i], 0, nf + j)),
            pl.BlockSpec((None, None, 1, tf), lambda j, i, be, nu: (li, be[i], 0, j)),
            pl.BlockSpec((None, None, 1, tf), lambda j, i, be, nu: (li, be[i], 0, nf + j)),
        ],
        out_specs=pl.BlockSpec((R, tf), lambda j, i, be, nu: (i, j)),
    )
    return pl.pallas_call(
        _moe_up_kernel,
        out_shape=jax.ShapeDtypeStruct((n_pad, D_FF), BF16),
        grid_spec=grid_spec,
        compiler_params=_cparams(("arbitrary", "arbitrary")),
        name="moe_up",
    )(block_e, n_used, x_rows, w_gu, w_gu, b_gu, b_gu)


def _moe_down_kernel(be_ref, nu_ref, a_ref, w_ref, b_ref, o_ref):
    i = pl.program_id(1)

    @pl.when(i < nu_ref[0])
    def _():
        o_ref[...] = jnp.dot(a_ref[...], w_ref[...].astype(BF16), preferred_element_type=F32) + b_ref[...]

    @pl.when(i >= nu_ref[0])
    def _():
        o_ref[...] = jnp.zeros_like(o_ref)


def _moe_down(act, w_down, b_down, block_e, n_used, li, tn=1024):
    n_pad, F = act.shape
    R = MOE_ROWS
    nb = n_pad // R
    D = D_MODEL

    def amap(j, i, be, nu):
        return (jnp.minimum(i, nu[0] - 1), 0)

    grid_spec = pltpu.PrefetchScalarGridSpec(
        num_scalar_prefetch=2,
        grid=(D // tn, nb),
        in_specs=[
            pl.BlockSpec((R, F), amap),
            pl.BlockSpec((None, None, F, tn), lambda j, i, be, nu: (li, be[i], 0, j)),
            pl.BlockSpec((None, None, 1, tn), lambda j, i, be, nu: (li, be[i], 0, j)),
        ],
        out_specs=pl.BlockSpec((R, tn), lambda j, i, be, nu: (i, j)),
    )
    return pl.pallas_call(
        _moe_down_kernel,
        out_shape=jax.ShapeDtypeStruct((n_pad, D), F32),
        grid_spec=grid_spec,
        compiler_params=_cparams(("arbitrary", "arbitrary")),
        name="moe_down",
    )(block_e, n_used, act, w_down, b_down)


def _combine_ln_kernel(tmc, dest_ref, y_ref, gate_ref, h_ref, g_ref, b_ref,
                       o_ref, obf_ref, buf, sem):
    def row_copy(t, k, d):
        return pltpu.make_async_copy(y_ref.at[pl.ds(d, 1), :], buf.at[k, pl.ds(t, 1), :], sem)

    def issue(t, c):
        for k in range(TOP_K):
            row_copy(t, k, dest_ref[t * TOP_K + k]).start()
        return c

    def drain(t, c):
        for k in range(TOP_K):
            row_copy(t, k, dest_ref[t * TOP_K + k]).wait()
        return c

    lax.fori_loop(0, tmc, issue, 0)
    lax.fori_loop(0, tmc, drain, 0)

    gate = gate_ref[...]
    ffn = gate[:, 0:1] * buf[0]
    for k in range(1, TOP_K):
        ffn = ffn + gate[:, k:k + 1] * buf[k]
    hn = _layer_norm(DEEPNORM_ALPHA * h_ref[...] + ffn, g_ref[...], b_ref[...])
    o_ref[...] = hn
    obf_ref[...] = hn.astype(BF16)


def _combine_ln(y_rows, dest_flat, gate, h, gain, bias, tmc=256):
    T, D = h.shape
    tmc = min(tmc, T)
    row = lambda i: (i, 0)
    fixed = lambda i: (0, 0)
    return pl.pallas_call(
        functools.partial(_combine_ln_kernel, tmc),
        out_shape=(jax.ShapeDtypeStruct((T, D), F32), jax.ShapeDtypeStruct((T, D), BF16)),
        grid=(T // tmc,),
        in_specs=[
            pl.BlockSpec((tmc * TOP_K,), lambda i: (i,), memory_space=pltpu.SMEM),
            pl.BlockSpec(memory_space=pl.ANY),
            pl.BlockSpec((tmc, LANES), row),
            pl.BlockSpec((tmc, D), row),
            pl.BlockSpec((1, D), fixed),
            pl.BlockSpec((1, D), fixed),
        ],
        out_specs=(pl.BlockSpec((tmc, D), row), pl.BlockSpec((tmc, D), row)),
        scratch_shapes=[pltpu.VMEM((TOP_K, tmc, D), F32), pltpu.SemaphoreType.DMA(())],
        compiler_params=_cparams(("arbitrary",)),
        name="moe_combine_ln",
    )(dest_flat, y_rows, gate, h, gain, bias)


def _moe_layer(h, topi, gate, rank, counts, li, w_gu, b_gu, w_down, b_down, gain, bias):
    T = h.shape[0]
    R = MOE_ROWS
    n_rows = T * TOP_K
    n_blocks = -(-(n_rows + N_EXPERTS * (R - 1)) // R)
    n_pad = n_blocks * R
    top_idx = topi[:, :TOP_K]
    counts = counts[0, :N_EXPERTS]
    padded = ((counts + R - 1) // R) * R
    pad_end = jnp.cumsum(padded)
    pad_start = pad_end - padded
    dest = (pad_start[top_idx] + rank[:, :TOP_K]).astype(I32).reshape(-1)
    block_start = jnp.arange(n_blocks, dtype=I32)[:, None] * R
    block_e = jnp.minimum(jnp.sum((pad_end[None, :] <= block_start).astype(I32), axis=1),
                          N_EXPERTS - 1).astype(I32)
    n_used = (pad_end[-1:] // R).astype(I32)
    fill = jnp.stack([jnp.concatenate([pad_start + counts, n_used]),
                      jnp.concatenate([pad_end, n_used])]).astype(I32)

    x_rows = _dispatch(h, dest, fill, n_pad)
    act = _moe_up(x_rows, w_gu, b_gu, block_e, n_used, li)
    y_rows = _moe_down(act, w_down, b_down, block_e, n_used, li)
    return _combine_ln(y_rows, dest, gate, h, gain, bias)


def _compress_one(c_ref, pos_ref, w1t_ref, w1b_ref, w2_ref, o_ref):
    c = c_ref[...]
    pos = pos_ref[...]
    a = jnp.dot((c + pos[0:1]).astype(BF16), w1t_ref[...], preferred_element_type=F32)
    b = jnp.dot((c + pos[1:2]).astype(BF16), w1b_ref[...], preferred_element_type=F32)
    n = a.shape[0]
    pre = a + pltpu.roll(b, n - 1, 0)
    hid = pre * jax.nn.sigmoid(pre)
    out = jnp.dot(hid.astype(BF16), w2_ref[...], preferred_element_type=F32)
    row = lax.broadcasted_iota(I32, out.shape, 0)
    o_ref[...] = jnp.where(row < n - 1, out, 0.0).astype(o_ref.dtype)


def _compress_kernel(ck_ref, cv_ref, pk_ref, pv_ref, k1t, k1b, k2, v1t, v1b, v2, ok_ref, ov_ref):
    _compress_one(ck_ref, pk_ref, k1t, k1b, k2, ok_ref)
    _compress_one(cv_ref, pv_ref, v1t, v1b, v2, ov_ref)


def _compress(ck, cv, pk, pv, k1t, k1b, k2, v1t, v1b, v2):
    B, G, NC, W = ck.shape
    blk = pl.BlockSpec((None, None, NC, W), lambda b, g: (b, g, 0, 0))
    oblk = pl.BlockSpec((None, None, NC, HEAD_DIM), lambda b, g: (b, g, 0, 0))
    full = lambda shape: pl.BlockSpec(shape, lambda b, g: (0, 0))
    return pl.pallas_call(
        _compress_kernel,
        out_shape=(jax.ShapeDtypeStruct((B, G, NC, HEAD_DIM), BF16),
                   jax.ShapeDtypeStruct((B, G, NC, HEAD_DIM), BF16)),
        grid=(B, G),
        in_specs=[blk, blk, full((2, W)), full((2, W)),
                  full((W, HEAD_DIM)), full((W, HEAD_DIM)), full((HEAD_DIM, HEAD_DIM)),
                  full((W, HEAD_DIM)), full((W, HEAD_DIM)), full((HEAD_DIM, HEAD_DIM))],
        out_specs=(oblk, oblk),
        compiler_params=_cparams(("arbitrary", "arbitrary")),
        name="nsa_compress",
    )(ck, cv, pk, pv, k1t, k1b, k2, v1t, v1b, v2)


def _nsa_attn_kernel(S, QB, tk, q_ref, kc_ref, vct_ref, ks_ref, kw_ref, vst_ref, vwt_ref,
                     oh_ref, gates_ref, impt_ref, o_ref, acc, qa, s_buf, ow_buf):
    qi = pl.program_id(2)
    NC = S // CMP_STRIDE
    s0 = qi * QB
    nt = (((1,), (1,)), ((), ()))
    rows = NSA_HPG * QB

    q = q_ref[...]
    qs = jnp.concatenate([q[:, h * HEAD_DIM:(h + 1) * HEAD_DIM] for h in range(NSA_HPG)], axis=0)
    t_row = s0 + lax.broadcasted_iota(I32, (1, QB), 1)

    def heads(m):
        return jnp.concatenate([m] * NSA_HPG, axis=1)

    sc = lax.dot_general(kc_ref[...], qs, nt, preferred_element_type=F32)
    cend = lax.broadcasted_iota(I32, (NC, QB), 0) * CMP_STRIDE + (CMP_LEN - 1)
    cbias = heads(jnp.where(cend <= t_row, 0.0, NEG))
    scm = sc + cbias
    mc = jnp.max(scm, axis=0, keepdims=True)
    pc = jnp.where(cbias == 0.0, jnp.exp2((scm - mc) * SM_SCALE_LOG2), 0.0)
    lc = jnp.sum(pc, axis=0, keepdims=True)
    pc = pc * jnp.where(lc > 0.0, 1.0 / jnp.maximum(lc, 1e-30), 0.0)
    o_cmp = jnp.dot(vct_ref[...], pc.astype(BF16), preferred_element_type=F32)

    pcs = pc[:, 0:QB]
    for h in range(1, NSA_HPG):
        pcs = pcs + pc[:, h * QB:(h + 1) * QB]
    imp_m = impt_ref[...]
    p_hi = pcs.astype(BF16)
    r1 = pcs - p_hi.astype(F32)
    p_mid = r1.astype(BF16)
    p_lo = (r1 - p_mid.astype(F32)).astype(BF16)
    imp_t = (jnp.dot(imp_m, p_hi, preferred_element_type=F32)
             + jnp.dot(imp_m, p_mid, preferred_element_type=F32)
             + jnp.dot(imp_m, p_lo, preferred_element_type=F32))

    WK = WINDOW + QB
    start = pl.multiple_of(jnp.maximum(s0 - WINDOW, 0), NSA_Q_BLOCK)
    sw = lax.dot_general(kw_ref[pl.ds(start, WK), :], qs, nt, preferred_element_type=F32)
    wpos = start + lax.broadcasted_iota(I32, (WK, QB), 0)
    wok = (wpos <= t_row) & (wpos > t_row - WINDOW)
    sw = sw + heads(jnp.where(wok, 0.0, NEG))
    mw = jnp.max(sw, axis=0, keepdims=True)
    pw = jnp.exp2((sw - mw) * SM_SCALE_LOG2)
    lw = jnp.sum(pw, axis=0, keepdims=True)
    ow_buf[...] = jnp.dot(vwt_ref[:, pl.ds(start, WK)], pw.astype(BF16),
                          preferred_element_type=F32) / lw

    blk_i = lax.broadcasted_iota(I32, (LANES, QB), 0)
    blk = blk_i.astype(F32)
    cur_blk = t_row // SEL_LEN
    valid = blk_i * SEL_LEN <= t_row
    forced = (blk_i == 0) | (blk_i == cur_blk) | (blk_i == cur_blk - 1)
    score = jnp.where(valid, jnp.where(forced, -2.0, imp_t), -1.0)
    sel_t = jnp.where(forced, 1.0, 0.0)
    for _ in range(N_SELECT - 3):
        m = jnp.max(score, axis=0, keepdims=True)
        idx = jnp.min(jnp.where(score == m, blk, float(LANES)), axis=0, keepdims=True)
        hit = blk == idx
        sel_t = jnp.where(hit, 1.0, sel_t)
        score = jnp.where(hit, -2.0, score)
    sel_bias = ((sel_t - 1.0) * (-NEG)).T.astype(BF16)
    qa[:, 0:HEAD_DIM] = qs
    qa[:, HEAD_DIM:2 * HEAD_DIM] = jnp.concatenate([sel_bias] * NSA_HPG, axis=0)

    acc[...] = jnp.zeros_like(acc)

    def sel_scores(kt, slot):
        off = pl.multiple_of(kt * tk, tk)
        k = jnp.concatenate([ks_ref[pl.ds(off, tk), :], oh_ref[pl.ds(off, tk), :]], axis=1)
        s_buf[slot] = lax.dot_general(k, qa[...], nt, preferred_element_type=F32)

    def sel_tile(kt, slot, carry, causal):
        m, l = carry
        off = pl.multiple_of(kt * tk, tk)
        vt = vst_ref[:, pl.ds(off, tk)]
        s = s_buf[slot]
        if causal:
            kpos = off + lax.broadcasted_iota(I32, (tk, QB), 0)
            s = s + heads(jnp.where(kpos <= t_row, 0.0, NEG))
        m_new = jnp.maximum(m, jnp.max(s, axis=0, keepdims=True))
        alpha = jnp.exp2((m - m_new) * SM_SCALE_LOG2)
        p = jnp.exp2((s - m_new) * SM_SCALE_LOG2)
        l_new = alpha * l + jnp.sum(p, axis=0, keepdims=True)
        acc[...] = alpha * acc[...] + jnp.dot(vt, p.astype(BF16), preferred_element_type=F32)
        return m_new, l_new

    def sel_pair(tt, carry):
        kt = 2 * tt
        sel_scores(kt + 1, 1)
        carry = sel_tile(kt, 0, carry, False)
        sel_scores(kt + 2, 0)
        return sel_tile(kt + 1, 1, carry, False)

    def sel_odd(_, carry):
        carry = sel_tile(n_full - 1, 0, carry, False)
        sel_scores(n_full, 0)
        return carry

    n_full = s0 // tk
    sel_scores(0, 0)
    carry = lax.fori_loop(0, n_full // 2, sel_pair,
                          (jnp.full((1, rows), NEG, F32), jnp.zeros((1, rows), F32)))
    carry = lax.fori_loop(0, n_full % 2, sel_odd, carry)
    _, l_sel = sel_tile(n_full, 0, carry, True)
    o_sel = acc[...] / l_sel
    o_win = ow_buf[...]

    gates_t = gates_ref[...].T
    for h in range(NSA_HPG):
        r = slice(h * QB, (h + 1) * QB)
        y = (o_cmp[:, r] * gates_t[3 * h:3 * h + 1] + o_sel[:, r] * gates_t[3 * h + 1:3 * h + 2]
             + o_win[:, r] * gates_t[3 * h + 2:3 * h + 3])
        o_ref[:, h * HEAD_DIM:(h + 1) * HEAD_DIM] = y.T.astype(o_ref.dtype)


def _nsa_attention(proj, kc, vc, gates, imp_mt, B, S, QB=256, tk=512):
    G = NSA_GROUPS
    nq = S // QB
    T = B * S
    NC = S // CMP_STRIDE
    gw = NSA_HPG * HEAD_DIM
    rows = NSA_HPG * QB
    kv = lambda base: pl.BlockSpec((S, HEAD_DIM), lambda b, g, q: (b, base + g))
    seq_on_lanes = lambda n: pl.BlockSpec((None, None, HEAD_DIM, n), lambda b, g, q: (b, g, 0, 0))
    onehot = (jnp.arange(S, dtype=I32)[:, None] // SEL_LEN
              == jnp.arange(LANES, dtype=I32)[None, :]).astype(BF16)

    def values_t(c0):
        return proj[:, c0:c0 + G * HEAD_DIM].reshape(B, S, G, HEAD_DIM).transpose(0, 2, 3, 1)

    return pl.pallas_call(
        functools.partial(_nsa_attn_kernel, S, QB, tk),
        out_shape=jax.ShapeDtypeStruct((T, NSA_HEADS * HEAD_DIM), BF16),
        grid=(B, G, nq),
        in_specs=[
            pl.BlockSpec((QB, gw), lambda b, g, q: (b * nq + q, g)),
            pl.BlockSpec((None, None, NC, HEAD_DIM), lambda b, g, q: (b, g, 0, 0)),
            seq_on_lanes(NC),
            kv(4 * G), kv(5 * G),
            seq_on_lanes(S), seq_on_lanes(S),
            pl.BlockSpec((S, LANES), lambda b, g, q: (0, 0)),
            pl.BlockSpec((QB, LANES), lambda b, g, q: (b * nq + q, g)),
            pl.BlockSpec((LANES, NC), lambda b, g, q: (0, 0)),
        ],
        out_specs=pl.BlockSpec((QB, gw), lambda b, g, q: (b * nq + q, g)),
        scratch_shapes=[pltpu.VMEM((HEAD_DIM, rows), F32), pltpu.VMEM((rows, 2 * HEAD_DIM), BF16),
                        pltpu.VMEM((2, tk, rows), F32), pltpu.VMEM((HEAD_DIM, rows), F32)],
        compiler_params=_cparams(("arbitrary", "arbitrary", "arbitrary")),
        name="nsa_attn",
    )(proj, kc, vc.transpose(0, 1, 3, 2), proj, proj, values_t(6 * G * HEAD_DIM),
      values_t(7 * G * HEAD_DIM), onehot, gates, imp_mt)


def _importance_matrix_t(NC):
    R = SEL_LEN // CMP_STRIDE
    C = CMP_LEN // CMP_STRIDE
    j = jnp.arange(LANES, dtype=I32)[:, None]
    i = jnp.arange(NC, dtype=I32)[None, :]
    w = jnp.zeros((LANES, NC), F32)
    for m in range(R):
        for n in range(C):
            w = w + jnp.where((i == R * j + m - n) & (i < NC - 1), 1.0, 0.0)
    return w.astype(BF16)


def _nsa_mixer(h_bf, cos2, sin2, w_in, ck_pos, ck_w1, ck_w2, cv_pos, cv_w1, cv_w2, B, S):
    D = D_MODEL
    G = NSA_GROUPS
    q_w = w_in[:, :D]
    kvw = w_in[:, D:D + 6 * NSA_KV].reshape(D, N_BRANCH, 2, NSA_KV)
    g_w = w_in[:, D + 6 * NSA_KV:].reshape(D, G, NSA_HPG * N_BRANCH)
    w_main = jnp.concatenate([q_w, kvw[:, 1, 0], kvw[:, 2, 0], kvw[:, 1, 1], kvw[:, 2, 1]], axis=1).astype(BF16)
    w_cmp = jnp.concatenate([kvw[:, 0, 0], kvw[:, 0, 1]], axis=1).astype(BF16)
    w_gate = jnp.pad(g_w, ((0, 0), (0, 0), (0, LANES - NSA_HPG * N_BRANCH))).reshape(D, G * LANES).astype(BF16)

    proj = _proj(h_bf, w_main, cos2, sin2, (D + 2 * NSA_KV) // 1024, BF16, tn=1024)
    cmp = _proj(h_bf, w_cmp, cos2, sin2, NSA_KV // 512, F32)
    gates = _gate_proj(h_bf, w_gate)

    NC = S // CMP_STRIDE
    W = CMP_STRIDE * HEAD_DIM

    def chunks(t):
        return t.reshape(B, NC, CMP_STRIDE, G, HEAD_DIM).transpose(0, 3, 1, 2, 4).reshape(B, G, NC, W)

    def halves(w1):
        return w1[:W].astype(BF16), w1[W:].astype(BF16)

    k1t, k1b = halves(ck_w1)
    v1t, v1b = halves(cv_w1)
    kc, vc = _compress(chunks(cmp[:, :NSA_KV]), chunks(cmp[:, NSA_KV:]),
                       ck_pos.reshape(2, W), cv_pos.reshape(2, W),
                       k1t, k1b, ck_w2.astype(BF16), v1t, v1b, cv_w2.astype(BF16))
    return _nsa_attention(proj, kc, vc, gates, _importance_matrix_t(NC), B, S)


def kernel(x, positions, ln_gain, ln_bias, diff_w_in, diff_lambda_q1, diff_lambda_k1, diff_lambda_q2, diff_lambda_k2, diff_subln_gain, diff_w_o, nsa_w_in, nsa_cmp_k_pos, nsa_cmp_k_w1, nsa_cmp_k_w2, nsa_cmp_v_pos, nsa_cmp_v_w1, nsa_cmp_v_w2, nsa_w_o, moe_router_w, moe_router_b, moe_w_gate_up, moe_b_gate_up, moe_w_down, moe_b_down):
    B, S, D = x.shape
    T = B * S
    assert D == D_MODEL and S % 1024 == 0 and S // SEL_LEN <= LANES

    inv = ROPE_THETA ** (-jnp.arange(0, HEAD_DIM, 2, dtype=F32) / HEAD_DIM)
    ang = positions.astype(F32).reshape(T, 1) * inv[None, :]
    cos2 = jnp.concatenate([jnp.cos(ang), jnp.cos(ang)], axis=1)
    sin2 = jnp.concatenate([-jnp.sin(ang), jnp.sin(ang)], axis=1)

    w_gu = moe_w_gate_up
    w_dn = moe_w_down
    b_gu = moe_b_gate_up.reshape(DEPTH, N_EXPERTS, 1, 2 * D_FF)
    b_dn = moe_b_down.reshape(DEPTH, N_EXPERTS, 1, D)
    rw = jnp.pad(moe_router_w, ((0, 0), (0, 0), (0, LANES - N_EXPERTS)))
    rw_hi = rw.astype(BF16)
    rw_lo = (rw - rw_hi.astype(F32)).astype(BF16)
    rb = jnp.pad(moe_router_b, ((0, 0), (0, LANES - N_EXPERTS))).reshape(DEPTH, 1, LANES)

    h = x.reshape(T, D)
    h_bf = h.astype(BF16)
    for li in range(DEPTH):
        j = li // 2
        if li % 2 == 0:
            lambda_init = 0.8 - 0.6 * math.exp(-0.3 * li)
            proj = _proj(h_bf, diff_w_in[j].astype(BF16), cos2, sin2,
                         4 * DIFF_HEADS * HEAD_DIM // 1024, BF16, tn=1024)
            lamv = jnp.stack([diff_lambda_q1[j], diff_lambda_k1[j], diff_lambda_q2[j], diff_lambda_k2[j]])
            mix = _diff_attention(proj, lamv, diff_subln_gain[j], lambda_init, B, S)
            w_o = diff_w_o[j].astype(BF16)
        else:
            mix = _nsa_mixer(h_bf, cos2, sin2, nsa_w_in[j], nsa_cmp_k_pos[j], nsa_cmp_k_w1[j], nsa_cmp_k_w2[j],
                             nsa_cmp_v_pos[j], nsa_cmp_v_w1[j], nsa_cmp_v_w2[j], B, S)
            w_o = nsa_w_o[j].astype(BF16)
        h, topi, gate, rank, counts = _out_ln_route(
            mix, w_o, h, ln_gain[li, 0].reshape(1, D), ln_bias[li, 0].reshape(1, D),
            rw_hi[li], rw_lo[li], rb[li])
        h, h_bf = _moe_layer(h, topi, gate, rank, counts, li, w_gu, b_gu, w_dn, b_dn,
                             ln_gain[li, 1].reshape(1, D), ln_bias[li, 1].reshape(1, D))
    return h.reshape(B, S, D)
```

```python
import functools
import math

import jax
import jax.numpy as jnp
from jax import lax
from jax.experimental import pallas as pl
from jax.experimental.pallas import tpu as pltpu

F32 = jnp.float32
BF16 = jnp.bfloat16
I32 = jnp.int32

D_MODEL = 2048
DEPTH = 2
HEAD_DIM = 128
ROPE_THETA = 10000.0

DIFF_HEADS = D_MODEL // (2 * HEAD_DIM)
DIFF_V_DIM = 2 * HEAD_DIM

NSA_HEADS = D_MODEL // HEAD_DIM
NSA_GROUPS = 4
NSA_HPG = NSA_HEADS // NSA_GROUPS
NSA_KV = NSA_GROUPS * HEAD_DIM
N_BRANCH = 3
CMP_LEN = 32
CMP_STRIDE = 16
SEL_LEN = 64
N_SELECT = 16
WINDOW = 512
NSA_Q_BLOCK = 128

N_EXPERTS = 32
TOP_K = 4
D_FF = D_MODEL
SWIGLU_LIMIT = 7.0
SWIGLU_ALPHA = 1.702

DEEPNORM_ALPHA = (2 * DEPTH) ** 0.25
LN_EPS = 1e-5
RMS_EPS = 1e-5
NEG = -1e30
LOG2E = 1.4426950408889634
SM_SCALE_LOG2 = (HEAD_DIM ** -0.5) * LOG2E

LANES = 128
VMEM_LIMIT = 56 * 1024 * 1024
MOE_ROWS = 512
ZERO_CHUNKS = (256, 128, 64, 32, 16, 8)


def _cparams(sem):
    return pltpu.CompilerParams(dimension_semantics=sem, vmem_limit_bytes=VMEM_LIMIT)


def _proj_kernel(n_rope_tiles, a_ref, w_ref, cos_ref, sin_ref, o_ref):
    j = pl.program_id(0)
    y = jnp.dot(a_ref[...], w_ref[...], preferred_element_type=F32)
    tn = y.shape[1]

    @pl.when(j < n_rope_tiles)
    def _():
        cos = cos_ref[...]
        sin = sin_ref[...]
        for c in range(tn // HEAD_DIM):
            t = y[:, c * HEAD_DIM:(c + 1) * HEAD_DIM]
            r = t * cos + pltpu.roll(t, HEAD_DIM // 2, 1) * sin
            o_ref[:, c * HEAD_DIM:(c + 1) * HEAD_DIM] = r.astype(o_ref.dtype)

    @pl.when(j >= n_rope_tiles)
    def _():
        o_ref[...] = y.astype(o_ref.dtype)


def _proj(a, w, cos2, sin2, n_rope_tiles, out_dtype, tm=1024, tn=512):
    M, K = a.shape
    N = w.shape[1]
    tm = min(tm, M)
    return pl.pallas_call(
        functools.partial(_proj_kernel, n_rope_tiles),
        out_shape=jax.ShapeDtypeStruct((M, N), out_dtype),
        grid=(N // tn, M // tm),
        in_specs=[
            pl.BlockSpec((tm, K), lambda j, i: (i, 0)),
            pl.BlockSpec((K, tn), lambda j, i: (0, j)),
            pl.BlockSpec((tm, HEAD_DIM), lambda j, i: (i, 0)),
            pl.BlockSpec((tm, HEAD_DIM), lambda j, i: (i, 0)),
        ],
        out_specs=pl.BlockSpec((tm, tn), lambda j, i: (i, j)),
        compiler_params=_cparams(("arbitrary", "arbitrary")),
        name="proj_rope",
    )(a, w, cos2, sin2)


def _gate_kernel(a_ref, w_ref, o_ref):
    y = jnp.dot(a_ref[...], w_ref[...], preferred_element_type=F32)
    o_ref[...] = jax.nn.sigmoid(y)


def _gate_proj(a, w, tm=1024):
    M, K = a.shape
    N = w.shape[1]
    tm = min(tm, M)
    return pl.pallas_call(
        _gate_kernel,
        out_shape=jax.ShapeDtypeStruct((M, N), F32),
        grid=(M // tm,),
        in_specs=[pl.BlockSpec((tm, K), lambda i: (i, 0)),
                  pl.BlockSpec((K, N), lambda i: (0, 0))],
        out_specs=pl.BlockSpec((tm, N), lambda i: (i, 0)),
        compiler_params=_cparams(("arbitrary",)),
        name="nsa_gates",
    )(a, w)


def _diff_attn_kernel(lambda_init, tq, tk, q1_ref, q2_ref, k1_ref, k2_ref, vt_ref,
                      lam_ref, g_ref, o_ref, acc1, acc2, s1_buf, s2_buf):
    qi = pl.program_id(2)
    s0 = qi * tq
    nt = (((1,), (1,)), ((), ()))
    lamv = lam_ref[...]
    lam = (jnp.exp(jnp.sum(lamv[0:1] * lamv[1:2], axis=1, keepdims=True))
           - jnp.exp(jnp.sum(lamv[2:3] * lamv[3:4], axis=1, keepdims=True)) + lambda_init)
    acc1[...] = jnp.zeros_like(acc1)
    acc2[...] = jnp.zeros_like(acc2)

    def scores(kt, slot):
        off = pl.multiple_of(kt * tk, tk)
        s1_buf[slot] = lax.dot_general(k1_ref[pl.ds(off, tk), :], q1_ref[...], nt,
                                       preferred_element_type=F32)
        s2_buf[slot] = lax.dot_general(k2_ref[pl.ds(off, tk), :], q2_ref[...], nt,
                                       preferred_element_type=F32)

    def consume(kt, slot, carry, masked):
        off = pl.multiple_of(kt * tk, tk)
        vt = vt_ref[:, pl.ds(off, tk)]

        def one(s_buf, m, l, acc):
            st = s_buf[slot]
            if masked:
                kpos = off + lax.broadcasted_iota(I32, st.shape, 0)
                qpos = s0 + lax.broadcasted_iota(I32, st.shape, 1)
                st = jnp.where(kpos <= qpos, st, NEG)
            m_new = jnp.maximum(m, jnp.max(st, axis=0, keepdims=True))
            alpha = jnp.exp2((m - m_new) * SM_SCALE_LOG2)
            p = jnp.exp2((st - m_new) * SM_SCALE_LOG2)
            l_new = alpha * l + jnp.sum(p, axis=0, keepdims=True)
            acc[...] = alpha * acc[...] + jnp.dot(vt, p.astype(BF16), preferred_element_type=F32)
            return m_new, l_new

        m1, l1, m2, l2 = carry
        m1, l1 = one(s1_buf, m1, l1, acc1)
        m2, l2 = one(s2_buf, m2, l2, acc2)
        return m1, l1, m2, l2

    def pair(tt, carry):
        kt = 2 * tt
        scores(kt + 1, 1)
        carry = consume(kt, 0, carry, False)
        scores(kt + 2, 0)
        return consume(kt + 1, 1, carry, False)

    def odd_tail(_, carry):
        carry = consume(n_full - 1, 0, carry, False)
        scores(n_full, 0)
        return carry

    neg = jnp.full((1, tq), NEG, F32)
    zero = jnp.zeros((1, tq), F32)
    n_full = s0 // tk
    scores(0, 0)
    carry = lax.fori_loop(0, n_full // 2, pair, (neg, zero, neg, zero))
    carry = lax.fori_loop(0, n_full % 2, odd_tail, carry)
    n_edge = max(1, tq // tk)
    for j in range(n_edge):
        if j + 1 < n_edge:
            scores(n_full + j + 1, (j + 1) % 2)
        carry = consume(n_full + j, j % 2, carry, True)
    _, l1, _, l2 = carry

    a = acc1[...] / l1 - lam * (acc2[...] / l2)
    ms = jnp.mean(a * a, axis=0, keepdims=True)
    o = a * lax.rsqrt(ms + RMS_EPS) * g_ref[...]
    o_ref[...] = (o * (1.0 - lambda_init)).T.astype(o_ref.dtype)


def _diff_attention(proj, lamv, subln_g, lambda_init, B, S, tq=1024, tk=1024):
    H = DIFF_HEADS
    tq = min(tq, S)
    nq = S // tq
    T = B * S
    assert tk % tq == 0 or tq % tk == 0
    v_t = proj[:, 4 * H * HEAD_DIM:].reshape(B, S, H, DIFF_V_DIM).transpose(0, 2, 3, 1)
    kern = functools.partial(_diff_attn_kernel, lambda_init, tq, tk)
    return pl.pallas_call(
        kern,
        out_shape=jax.ShapeDtypeStruct((T, H * DIFF_V_DIM), BF16),
        grid=(B, H, nq),
        in_specs=[
            pl.BlockSpec((tq, HEAD_DIM), lambda b, h, q: (b * nq + q, h)),
            pl.BlockSpec((tq, HEAD_DIM), lambda b, h, q: (b * nq + q, H + h)),
            pl.BlockSpec((S, HEAD_DIM), lambda b, h, q: (b, 2 * H + h)),
            pl.BlockSpec((S, HEAD_DIM), lambda b, h, q: (b, 3 * H + h)),
            pl.BlockSpec((None, None, DIFF_V_DIM, S), lambda b, h, q: (b, h, 0, 0)),
            pl.BlockSpec((4, HEAD_DIM), lambda b, h, q: (0, 0)),
            pl.BlockSpec((DIFF_V_DIM, 1), lambda b, h, q: (0, 0)),
        ],
        out_specs=pl.BlockSpec((tq, DIFF_V_DIM), lambda b, h, q: (b * nq + q, h)),
        scratch_shapes=[pltpu.VMEM((DIFF_V_DIM, tq), F32), pltpu.VMEM((DIFF_V_DIM, tq), F32),
                        pltpu.VMEM((2, tk, tq), F32), pltpu.VMEM((2, tk, tq), F32)],
        compiler_params=_cparams(("arbitrary", "arbitrary", "arbitrary")),
        name="diff_attn",
    )(proj, proj, proj, proj, v_t, lamv, subln_g.reshape(DIFF_V_DIM, 1))


def _layer_norm(z, g, b):
    mu = jnp.mean(z, axis=-1, keepdims=True)
    zc = z - mu
    var = jnp.mean(zc * zc, axis=-1, keepdims=True)
    return zc * lax.rsqrt(var + LN_EPS) * g + b


def _route(hn, rw_hi_ref, rw_lo_ref, rb_ref, topi_ref, gate_ref, rank_ref, counts_ref, run_ref):
    hi = hn.astype(BF16)
    lo = (hn - hi.astype(F32)).astype(BF16)
    w_hi = rw_hi_ref[...]
    logits = (jnp.dot(hi, w_hi, preferred_element_type=F32)
              + jnp.dot(lo, w_hi, preferred_element_type=F32)
              + jnp.dot(hi, rw_lo_ref[...], preferred_element_type=F32)) + rb_ref[...]
    lane = lax.broadcasted_iota(I32, logits.shape, 1).astype(F32)
    cur = jnp.where(lane < N_EXPERTS, logits, -jnp.inf)
    vals, idxs = [], []
    for _ in range(TOP_K):
        m = jnp.max(cur, axis=1, keepdims=True)
        idx = jnp.min(jnp.where(cur == m, lane, float(LANES)), axis=1, keepdims=True)
        vals.append(m)
        idxs.append(idx)
        cur = jnp.where(lane == idx, -jnp.inf, cur)
    es = [jnp.exp(v - vals[0]) for v in vals]
    den = es[0] + es[1] + es[2] + es[3]
    gate_out = jnp.zeros_like(logits)
    idx_out = jnp.zeros_like(logits)
    for k in range(TOP_K):
        gate_out = jnp.where(lane == float(k), es[k] / den, gate_out)
        idx_out = jnp.where(lane == float(k), idxs[k], idx_out)
    gate_ref[...] = gate_out
    topi_ref[...] = idx_out.astype(I32)

    @pl.when(pl.program_id(0) == 0)
    def _():
        run_ref[...] = jnp.zeros_like(run_ref)

    tm = logits.shape[0]
    onehot = jnp.zeros_like(logits)
    for k in range(TOP_K):
        onehot = jnp.where(lane == idxs[k], 1.0, onehot)
    r_i = lax.broadcasted_iota(I32, (tm, tm), 0)
    c_i = lax.broadcasted_iota(I32, (tm, tm), 1)
    tri = jnp.where(r_i > c_i, 1.0, 0.0).astype(BF16)
    pos = jnp.dot(tri, onehot.astype(BF16), preferred_element_type=F32) + run_ref[...]
    rank_out = jnp.zeros_like(logits)
    for k in range(TOP_K):
        rk = jnp.sum(jnp.where(lane == idxs[k], pos, 0.0), axis=1, keepdims=True)
        rank_out = jnp.where(lane == float(k), rk, rank_out)
    rank_ref[...] = rank_out.astype(I32)
    run_ref[...] = run_ref[...] + jnp.sum(onehot, axis=0, keepdims=True)
    counts_ref[...] = run_ref[...].astype(I32)


def _out_ln_kernel(a_ref, w_ref, h_ref, g_ref, b_ref, rw_hi_ref, rw_lo_ref, rb_ref,
                   o_ref, topi_ref, gate_ref, rank_ref, counts_ref, run_ref):
    mix = jnp.dot(a_ref[...], w_ref[...], preferred_element_type=F32)
    hn = _layer_norm(DEEPNORM_ALPHA * h_ref[...] + mix, g_ref[...], b_ref[...])
    o_ref[...] = hn
    _route(hn, rw_hi_ref, rw_lo_ref, rb_ref, topi_ref, gate_ref, rank_ref, counts_ref, run_ref)


def _out_ln_route(a, w_o, h, gain, bias, rw_hi, rw_lo, rb, tm=512):
    T, K = a.shape
    D = w_o.shape[1]
    tm = min(tm, T)
    row = lambda i: (i, 0)
    fixed = lambda i: (0, 0)
    return pl.pallas_call(
        _out_ln_kernel,
        out_shape=(jax.ShapeDtypeStruct((T, D), F32),
                   jax.ShapeDtypeStruct((T, LANES), I32),
                   jax.ShapeDtypeStruct((T, LANES), F32),
                   jax.ShapeDtypeStruct((T, LANES), I32),
                   jax.ShapeDtypeStruct((1, LANES), I32)),
        grid=(T // tm,),
        in_specs=[
            pl.BlockSpec((tm, K), row),
            pl.BlockSpec((K, D), fixed),
            pl.BlockSpec((tm, D), row),
            pl.BlockSpec((1, D), fixed),
            pl.BlockSpec((1, D), fixed),
            pl.BlockSpec((D, LANES), fixed),
            pl.BlockSpec((D, LANES), fixed),
            pl.BlockSpec((1, LANES), fixed),
        ],
        out_specs=(pl.BlockSpec((tm, D), row), pl.BlockSpec((tm, LANES), row),
                   pl.BlockSpec((tm, LANES), row), pl.BlockSpec((tm, LANES), row),
                   pl.BlockSpec((1, LANES), fixed)),
        scratch_shapes=[pltpu.VMEM((1, LANES), F32)],
        compiler_params=_cparams(("arbitrary",)),
        name="out_ln_route",
    )(a, w_o, h, gain, bias, rw_hi, rw_lo, rb)


def _dispatch_kernel(tmd, dest_ref, fill_ref, h_ref, xout_ref, zrow, sem):
    def row_copy(t, d):
        return pltpu.make_async_copy(h_ref.at[pl.ds(t, 1), :], xout_ref.at[pl.ds(d, 1), :], sem)

    def zero_copy(d):
        return pltpu.make_async_copy(zrow.at[pl.ds(0, 1), :], xout_ref.at[pl.ds(d, 1), :], sem)

    def zero_block(b):
        r0 = pl.multiple_of(b * MOE_ROWS, MOE_ROWS)
        return pltpu.make_async_copy(zrow, xout_ref.at[pl.ds(r0, MOE_ROWS), :], sem)

    @pl.when(pl.program_id(0) == 0)
    def _():
        zrow[...] = jnp.zeros_like(zrow)
        n_used = fill_ref[0, N_EXPERTS]
        n_blocks = xout_ref.shape[0] // MOE_ROWS

        def pad_copies(e, act):
            lo = fill_ref[0, e]
            end = fill_ref[1, e]
            n = end - lo
            for p in ZERO_CHUNKS:
                take = (n & p) != 0
                r0 = pl.multiple_of(jnp.where(take, end - p, 0), p)

                @pl.when(take)
                def _():
                    act(pltpu.make_async_copy(zrow.at[pl.ds(0, p), :], xout_ref.at[pl.ds(r0, p), :], sem))

                end = jnp.where(take, end - p, end)
            lax.fori_loop(lo, lo + (n & (ZERO_CHUNKS[-1] - 1)), lambda d, cc: (act(zero_copy(d)), cc)[1], 0)

        def per_expert(e, c):
            pad_copies(e, lambda cp: cp.start())
            return c

        def per_expert_wait(e, c):
            pad_copies(e, lambda cp: cp.wait())
            return c

        lax.fori_loop(0, N_EXPERTS, per_expert, 0)
        lax.fori_loop(n_used, n_blocks, lambda b, cc: (zero_block(b).start(), cc)[1], 0)
        lax.fori_loop(0, N_EXPERTS, per_expert_wait, 0)
        lax.fori_loop(n_used, n_blocks, lambda b, cc: (zero_block(b).wait(), cc)[1], 0)

    def issue(t, c):
        for k in range(TOP_K):
            row_copy(t, dest_ref[t * TOP_K + k]).start()
        return c

    def drain(t, c):
        for k in range(TOP_K):
            row_copy(t, dest_ref[t * TOP_K + k]).wait()
        return c

    lax.fori_loop(0, tmd, issue, 0)
    lax.fori_loop(0, tmd, drain, 0)


def _dispatch(h, dest_flat, fill, n_pad, tmd=512):
    T, D = h.shape
    tmd = min(tmd, T)
    return pl.pallas_call(
        functools.partial(_dispatch_kernel, tmd),
        out_shape=jax.ShapeDtypeStruct((n_pad, D), h.dtype),
        grid=(T // tmd,),
        in_specs=[
            pl.BlockSpec((tmd * TOP_K,), lambda i: (i,), memory_space=pltpu.SMEM),
            pl.BlockSpec(memory_space=pltpu.SMEM),
            pl.BlockSpec((tmd, D), lambda i: (i, 0)),
        ],
        out_specs=pl.BlockSpec(memory_space=pl.ANY),
        scratch_shapes=[pltpu.VMEM((MOE_ROWS, D), h.dtype), pltpu.SemaphoreType.DMA(())],
        compiler_params=_cparams(("arbitrary",)),
        name="moe_dispatch",
    )(dest_flat, fill, h)


def _moe_up_kernel(be_ref, nu_ref, x_ref, wg_ref, wu_ref, bg_ref, bu_ref, o_ref):
    i = pl.program_id(1)

    @pl.when(i < nu_ref[0])
    def _():
        x = x_ref[...].astype(BF16)
        g = jnp.dot(x, wg_ref[...].astype(BF16), preferred_element_type=F32) + bg_ref[...]
        u = jnp.dot(x, wu_ref[...].astype(BF16), preferred_element_type=F32) + bu_ref[...]
        g = jnp.minimum(g, SWIGLU_LIMIT)
        u = jnp.clip(u, -SWIGLU_LIMIT, SWIGLU_LIMIT)
        act = g * jax.nn.sigmoid(SWIGLU_ALPHA * g) * (u + 1.0)
        o_ref[...] = act.astype(o_ref.dtype)

    @pl.when(i >= nu_ref[0])
    def _():
        o_ref[...] = jnp.zeros_like(o_ref)


def _moe_up(x_rows, w_gu, b_gu, block_e, n_used, li, tf=1024):
    n_pad, D = x_rows.shape
    R = MOE_ROWS
    nb = n_pad // R
    nf = D_FF // tf

    def xmap(j, i, be, nu):
        return (jnp.minimum(i, nu[0] - 1), 0)

    grid_spec = pltpu.PrefetchScalarGridSpec(
        num_scalar_prefetch=2,
        grid=(nf, nb),
        in_specs=[
            pl.BlockSpec((R, D), xmap),
            pl.BlockSpec((None, None, D, tf), lambda j, i, be, nu: (li, be[i], 0, j)),
            pl.BlockSpec((None, None, D, tf), lambda j, i, be, nu: (li, be[i], 0, nf + j)),
            pl.BlockSpec((None, None, 1, tf), lambda j, i, be, nu: (li, be[i], 0, j)),
            pl.BlockSpec((None, None, 1, tf), lambda j, i, be, nu: (li, be[i], 0, nf + j)),
        ],
        out_specs=pl.BlockSpec((R, tf), lambda j, i, be, nu: (i, j)),
    )
    return pl.pallas_call(
        _moe_up_kernel,
        out_shape=jax.ShapeDtypeStruct((n_pad, D_FF), BF16),
        grid_spec=grid_spec,
        compiler_params=_cparams(("arbitrary", "arbitrary")),
        name="moe_up",
    )(block_e, n_used, x_rows, w_gu, w_gu, b_gu, b_gu)


def _moe_down_kernel(be_ref, nu_ref, a_ref, w_ref, b_ref, o_ref):
    i = pl.program_id(1)

    @pl.when(i < nu_ref[0])
    def _():
        o_ref[...] = jnp.dot(a_ref[...], w_ref[...].astype(BF16), preferred_element_type=F32) + b_ref[...]

    @pl.when(i >= nu_ref[0])
    def _():
        o_ref[...] = jnp.zeros_like(o_ref)


def _moe_down(act, w_down, b_down, block_e, n_used, li, tn=1024):
    n_pad, F = act.shape
    R = MOE_ROWS
    nb = n_pad // R
    D = D_MODEL

    def amap(j, i, be, nu):
        return (jnp.minimum(i, nu[0] - 1), 0)

    grid_spec = pltpu.PrefetchScalarGridSpec(
        num_scalar_prefetch=2,
        grid=(D // tn, nb),
        in_specs=[
            pl.BlockSpec((R, F), amap),
            pl.BlockSpec((None, None, F, tn), lambda j, i, be, nu: (li, be[i], 0, j)),
            pl.BlockSpec((None, None, 1, tn), lambda j, i, be, nu: (li, be[i], 0, j)),
        ],
        out_specs=pl.BlockSpec((R, tn), lambda j, i, be, nu: (i, j)),
    )
    return pl.pallas_call(
        _moe_down_kernel,
        out_shape=jax.ShapeDtypeStruct((n_pad, D), F32),
        grid_spec=grid_spec,
        compiler_params=_cparams(("arbitrary", "arbitrary")),
        name="moe_down",
    )(block_e, n_used, act, w_down, b_down)


def _combine_ln_kernel(tmc, dest_ref, y_ref, gate_ref, h_ref, g_ref, b_ref,
                       o_ref, obf_ref, buf, sem):
    def row_copy(t, k, d):
        return pltpu.make_async_copy(y_ref.at[pl.ds(d, 1), :], buf.at[k, pl.ds(t, 1), :], sem)

    def issue(t, c):
        for k in range(TOP_K):
            row_copy(t, k, dest_ref[t * TOP_K + k]).start()
        return c

    def drain(t, c):
        for k in range(TOP_K):
            row_copy(t, k, dest_ref[t * TOP_K + k]).wait()
        return c

    lax.fori_loop(0, tmc, issue, 0)
    lax.fori_loop(0, tmc, drain, 0)

    gate = gate_ref[...]
    ffn = gate[:, 0:1] * buf[0]
    for k in range(1, TOP_K):
        ffn = ffn + gate[:, k:k + 1] * buf[k]
    hn = _layer_norm(DEEPNORM_ALPHA * h_ref[...] + ffn, g_ref[...], b_ref[...])
    o_ref[...] = hn
    obf_ref[...] = hn.astype(BF16)


def _combine_ln(y_rows, dest_flat, gate, h, gain, bias, tmc=512):
    T, D = h.shape
    tmc = min(tmc, T)
    row = lambda i: (i, 0)
    fixed = lambda i: (0, 0)
    return pl.pallas_call(
        functools.partial(_combine_ln_kernel, tmc),
        out_shape=(jax.ShapeDtypeStruct((T, D), F32), jax.ShapeDtypeStruct((T, D), BF16)),
        grid=(T // tmc,),
        in_specs=[
            pl.BlockSpec((tmc * TOP_K,), lambda i: (i,), memory_space=pltpu.SMEM),
            pl.BlockSpec(memory_space=pl.ANY),
            pl.BlockSpec((tmc, LANES), row),
            pl.BlockSpec((tmc, D), row),
            pl.BlockSpec((1, D), fixed),
            pl.BlockSpec((1, D), fixed),
        ],
        out_specs=(pl.BlockSpec((tmc, D), row), pl.BlockSpec((tmc, D), row)),
        scratch_shapes=[pltpu.VMEM((TOP_K, tmc, D), F32), pltpu.SemaphoreType.DMA(())],
        compiler_params=_cparams(("arbitrary",)),
        name="moe_combine_ln",
    )(dest_flat, y_rows, gate, h, gain, bias)


def _moe_layer(h, topi, gate, rank, counts, li, w_gu, b_gu, w_down, b_down, gain, bias):
    T = h.shape[0]
    R = MOE_ROWS
    n_rows = T * TOP_K
    n_blocks = -(-(n_rows + N_EXPERTS * (R - 1)) // R)
    n_pad = n_blocks * R
    top_idx = topi[:, :TOP_K]
    counts = counts[0, :N_EXPERTS]
    padded = ((counts + R - 1) // R) * R
    pad_end = jnp.cumsum(padded)
    pad_start = pad_end - padded
    dest = (pad_start[top_idx] + rank[:, :TOP_K]).astype(I32).reshape(-1)
    block_start = jnp.arange(n_blocks, dtype=I32)[:, None] * R
    block_e = jnp.minimum(jnp.sum((pad_end[None, :] <= block_start).astype(I32), axis=1),
                          N_EXPERTS - 1).astype(I32)
    n_used = (pad_end[-1:] // R).astype(I32)
    fill = jnp.stack([jnp.concatenate([pad_start + counts, n_used]),
                      jnp.concatenate([pad_end, n_used])]).astype(I32)

    x_rows = _dispatch(h, dest, fill, n_pad)
    act = _moe_up(x_rows, w_gu, b_gu, block_e, n_used, li)
    y_rows = _moe_down(act, w_down, b_down, block_e, n_used, li)
    return _combine_ln(y_rows, dest, gate, h, gain, bias)


def _compress_one(c_ref, pos_ref, w1t_ref, w1b_ref, w2_ref, o_ref):
    c = c_ref[...]
    pos = pos_ref[...]
    a = jnp.dot((c + pos[0:1]).astype(BF16), w1t_ref[...], preferred_element_type=F32)
    b = jnp.dot((c + pos[1:2]).astype(BF16), w1b_ref[...], preferred_element_type=F32)
    n = a.shape[0]
    pre = a + pltpu.roll(b, n - 1, 0)
    hid = pre * jax.nn.sigmoid(pre)
    out = jnp.dot(hid.astype(BF16), w2_ref[...], preferred_element_type=F32)
    row = lax.broadcasted_iota(I32, out.shape, 0)
    o_ref[...] = jnp.where(row < n - 1, out, 0.0).astype(o_ref.dtype)


def _compress_kernel(ck_ref, cv_ref, pk_ref, pv_ref, k1t, k1b, k2, v1t, v1b, v2, ok_ref, ov_ref):
    _compress_one(ck_ref, pk_ref, k1t, k1b, k2, ok_ref)
    _compress_one(cv_ref, pv_ref, v1t, v1b, v2, ov_ref)


def _compress(ck, cv, pk, pv, k1t, k1b, k2, v1t, v1b, v2):
    B, G, NC, W = ck.shape
    blk = pl.BlockSpec((None, None, NC, W), lambda b, g: (b, g, 0, 0))
    oblk = pl.BlockSpec((None, None, NC, HEAD_DIM), lambda b, g: (b, g, 0, 0))
    full = lambda shape: pl.BlockSpec(shape, lambda b, g: (0, 0))
    return pl.pallas_call(
        _compress_kernel,
        out_shape=(jax.ShapeDtypeStruct((B, G, NC, HEAD_DIM), BF16),
                   jax.ShapeDtypeStruct((B, G, NC, HEAD_DIM), BF16)),
        grid=(B, G),
        in_specs=[blk, blk, full((2, W)), full((2, W)),
                  full((W, HEAD_DIM)), full((W, HEAD_DIM)), full((HEAD_DIM, HEAD_DIM)),
                  full((W, HEAD_DIM)), full((W, HEAD_DIM)), full((HEAD_DIM, HEAD_DIM))],
        out_specs=(oblk, oblk),
        compiler_params=_cparams(("arbitrary", "arbitrary")),
        name="nsa_compress",
    )(ck, cv, pk, pv, k1t, k1b, k2, v1t, v1b, v2)


def _nsa_attn_kernel(S, QB, tk, q_ref, kc_ref, vct_ref, ks_ref, kw_ref, vst_ref, vwt_ref,
                     oh_ref, gates_ref, impt_ref, o_ref, acc, qa, s_buf, ow_buf):
    qi = pl.program_id(2)
    NC = S // CMP_STRIDE
    s0 = qi * QB
    nt = (((1,), (1,)), ((), ()))
    rows = NSA_HPG * QB

    q = q_ref[...]
    qs = jnp.concatenate([q[:, h * HEAD_DIM:(h + 1) * HEAD_DIM] for h in range(NSA_HPG)], axis=0)
    t_row = s0 + lax.broadcasted_iota(I32, (1, QB), 1)

    def heads(m):
        return jnp.concatenate([m] * NSA_HPG, axis=1)

    sc = lax.dot_general(kc_ref[...], qs, nt, preferred_element_type=F32)
    cend = lax.broadcasted_iota(I32, (NC, QB), 0) * CMP_STRIDE + (CMP_LEN - 1)
    cbias = heads(jnp.where(cend <= t_row, 0.0, NEG))
    scm = sc + cbias
    mc = jnp.max(scm, axis=0, keepdims=True)
    pc = jnp.where(cbias == 0.0, jnp.exp2((scm - mc) * SM_SCALE_LOG2), 0.0)
    lc = jnp.sum(pc, axis=0, keepdims=True)
    pc = pc * jnp.where(lc > 0.0, 1.0 / jnp.maximum(lc, 1e-30), 0.0)
    o_cmp = jnp.dot(vct_ref[...], pc.astype(BF16), preferred_element_type=F32)

    pcs = pc[:, 0:QB]
    for h in range(1, NSA_HPG):
        pcs = pcs + pc[:, h * QB:(h + 1) * QB]
    imp_m = impt_ref[...]
    p_hi = pcs.astype(BF16)
    r1 = pcs - p_hi.astype(F32)
    p_mid = r1.astype(BF16)
    p_lo = (r1 - p_mid.astype(F32)).astype(BF16)
    imp_t = (jnp.dot(imp_m, p_hi, preferred_element_type=F32)
             + jnp.dot(imp_m, p_mid, preferred_element_type=F32)
             + jnp.dot(imp_m, p_lo, preferred_element_type=F32))

    WK = WINDOW + QB
    start = pl.multiple_of(jnp.maximum(s0 - WINDOW, 0), NSA_Q_BLOCK)
    sw = lax.dot_general(kw_ref[pl.ds(start, WK), :], qs, nt, preferred_element_type=F32)
    wpos = start + lax.broadcasted_iota(I32, (WK, QB), 0)
    wok = (wpos <= t_row) & (wpos > t_row - WINDOW)
    sw = sw + heads(jnp.where(wok, 0.0, NEG))
    mw = jnp.max(sw, axis=0, keepdims=True)
    pw = jnp.exp2((sw - mw) * SM_SCALE_LOG2)
    lw = jnp.sum(pw, axis=0, keepdims=True)
    ow_buf[...] = jnp.dot(vwt_ref[:, pl.ds(start, WK)], pw.astype(BF16),
                          preferred_element_type=F32) / lw

    blk_i = lax.broadcasted_iota(I32, (LANES, QB), 0)
    blk = blk_i.astype(F32)
    cur_blk = t_row // SEL_LEN
    valid = blk_i * SEL_LEN <= t_row
    forced = (blk_i == 0) | (blk_i == cur_blk) | (blk_i == cur_blk - 1)
    score = jnp.where(valid, jnp.where(forced, -2.0, imp_t), -1.0)
    sel_t = jnp.where(forced, 1.0, 0.0)
    for _ in range(N_SELECT - 3):
        m = jnp.max(score, axis=0, keepdims=True)
        idx = jnp.min(jnp.where(score == m, blk, float(LANES)), axis=0, keepdims=True)
        hit = blk == idx
        sel_t = jnp.where(hit, 1.0, sel_t)
        score = jnp.where(hit, -2.0, score)
    sel_bias = ((sel_t - 1.0) * (-NEG)).T.astype(BF16)
    qa[:, 0:HEAD_DIM] = qs
    qa[:, HEAD_DIM:2 * HEAD_DIM] = jnp.concatenate([sel_bias] * NSA_HPG, axis=0)

    acc[...] = jnp.zeros_like(acc)

    def sel_scores(kt, slot):
        off = pl.multiple_of(kt * tk, tk)
        k = jnp.concatenate([ks_ref[pl.ds(off, tk), :], oh_ref[pl.ds(off, tk), :]], axis=1)
        s_buf[slot] = lax.dot_general(k, qa[...], nt, preferred_element_type=F32)

    def sel_tile(kt, slot, carry, causal):
        m, l = carry
        off = pl.multiple_of(kt * tk, tk)
        vt = vst_ref[:, pl.ds(off, tk)]
        s = s_buf[slot]
        if causal:
            kpos = off + lax.broadcasted_iota(I32, (tk, QB), 0)
            s = s + heads(jnp.where(kpos <= t_row, 0.0, NEG))
        m_new = jnp.maximum(m, jnp.max(s, axis=0, keepdims=True))
        alpha = jnp.exp2((m - m_new) * SM_SCALE_LOG2)
        p = jnp.exp2((s - m_new) * SM_SCALE_LOG2)
        l_new = alpha * l + jnp.sum(p, axis=0, keepdims=True)
        acc[...] = alpha * acc[...] + jnp.dot(vt, p.astype(BF16), preferred_element_type=F32)
        return m_new, l_new

    def sel_pair(tt, carry):
        kt = 2 * tt
        sel_scores(kt + 1, 1)
        carry = sel_tile(kt, 0, carry, False)
        sel_scores(kt + 2, 0)
        return sel_tile(kt + 1, 1, carry, False)

    def sel_odd(_, carry):
        carry = sel_tile(n_full - 1, 0, carry, False)
        sel_scores(n_full, 0)
        return carry

    n_full = s0 // tk
    sel_scores(0, 0)
    carry = lax.fori_loop(0, n_full // 2, sel_pair,
                          (jnp.full((1, rows), NEG, F32), jnp.zeros((1, rows), F32)))
    carry = lax.fori_loop(0, n_full % 2, sel_odd, carry)
    _, l_sel = sel_tile(n_full, 0, carry, True)
    o_sel = acc[...] / l_sel
    o_win = ow_buf[...]

    gates_t = gates_ref[...].T
    for h in range(NSA_HPG):
        r = slice(h * QB, (h + 1) * QB)
        y = (o_cmp[:, r] * gates_t[3 * h:3 * h + 1] + o_sel[:, r] * gates_t[3 * h + 1:3 * h + 2]
             + o_win[:, r] * gates_t[3 * h + 2:3 * h + 3])
        o_ref[:, h * HEAD_DIM:(h + 1) * HEAD_DIM] = y.T.astype(o_ref.dtype)


def _nsa_attention(proj, kc, vc, gates, imp_mt, B, S, QB=256, tk=512):
    G = NSA_GROUPS
    nq = S // QB
    T = B * S
    NC = S // CMP_STRIDE
    gw = NSA_HPG * HEAD_DIM
    rows = NSA_HPG * QB
    kv = lambda base: pl.BlockSpec((S, HEAD_DIM), lambda b, g, q: (b, base + g))
    seq_on_lanes = lambda n: pl.BlockSpec((None, None, HEAD_DIM, n), lambda b, g, q: (b, g, 0, 0))
    onehot = (jnp.arange(S, dtype=I32)[:, None] // SEL_LEN
              == jnp.arange(LANES, dtype=I32)[None, :]).astype(BF16)

    def values_t(c0):
        return proj[:, c0:c0 + G * HEAD_DIM].reshape(B, S, G, HEAD_DIM).transpose(0, 2, 3, 1)

    return pl.pallas_call(
        functools.partial(_nsa_attn_kernel, S, QB, tk),
        out_shape=jax.ShapeDtypeStruct((T, NSA_HEADS * HEAD_DIM), BF16),
        grid=(B, G, nq),
        in_specs=[
            pl.BlockSpec((QB, gw), lambda b, g, q: (b * nq + q, g)),
            pl.BlockSpec((None, None, NC, HEAD_DIM), lambda b, g, q: (b, g, 0, 0)),
            seq_on_lanes(NC),
            kv(4 * G), kv(5 * G),
            seq_on_lanes(S), seq_on_lanes(S),
            pl.BlockSpec((S, LANES), lambda b, g, q: (0, 0)),
            pl.BlockSpec((QB, LANES), lambda b, g, q: (b * nq + q, g)),
            pl.BlockSpec((LANES, NC), lambda b, g, q: (0, 0)),
        ],
        out_specs=pl.BlockSpec((QB, gw), lambda b, g, q: (b * nq + q, g)),
        scratch_shapes=[pltpu.VMEM((HEAD_DIM, rows), F32), pltpu.VMEM((rows, 2 * HEAD_DIM), BF16),
                        pltpu.VMEM((2, tk, rows), F32), pltpu.VMEM((HEAD_DIM, rows), F32)],
        compiler_params=_cparams(("arbitrary", "arbitrary", "arbitrary")),
        name="nsa_attn",
    )(proj, kc, vc.transpose(0, 1, 3, 2), proj, proj, values_t(6 * G * HEAD_DIM),
      values_t(7 * G * HEAD_DIM), onehot, gates, imp_mt)


def _importance_matrix_t(NC):
    R = SEL_LEN // CMP_STRIDE
    C = CMP_LEN // CMP_STRIDE
    j = jnp.arange(LANES, dtype=I32)[:, None]
    i = jnp.arange(NC, dtype=I32)[None, :]
    w = jnp.zeros((LANES, NC), F32)
    for m in range(R):
        for n in range(C):
            w = w + jnp.where((i == R * j + m - n) & (i < NC - 1), 1.0, 0.0)
    return w.astype(BF16)


def _nsa_mixer(h_bf, cos2, sin2, w_in, ck_pos, ck_w1, ck_w2, cv_pos, cv_w1, cv_w2, B, S):
    D = D_MODEL
    G = NSA_GROUPS
    q_w = w_in[:, :D]
    kvw = w_in[:, D:D + 6 * NSA_KV].reshape(D, N_BRANCH, 2, NSA_KV)
    g_w = w_in[:, D + 6 * NSA_KV:].reshape(D, G, NSA_HPG * N_BRANCH)
    w_main = jnp.concatenate([q_w, kvw[:, 1, 0], kvw[:, 2, 0], kvw[:, 1, 1], kvw[:, 2, 1]], axis=1).astype(BF16)
    w_cmp = jnp.concatenate([kvw[:, 0, 0], kvw[:, 0, 1]], axis=1).astype(BF16)
    w_gate = jnp.pad(g_w, ((0, 0), (0, 0), (0, LANES - NSA_HPG * N_BRANCH))).reshape(D, G * LANES).astype(BF16)

    proj = _proj(h_bf, w_main, cos2, sin2, (D + 2 * NSA_KV) // 1024, BF16, tn=1024)
    cmp = _proj(h_bf, w_cmp, cos2, sin2, NSA_KV // 512, F32)
    gates = _gate_proj(h_bf, w_gate)

    NC = S // CMP_STRIDE
    W = CMP_STRIDE * HEAD_DIM

    def chunks(t):
        return t.reshape(B, NC, CMP_STRIDE, G, HEAD_DIM).transpose(0, 3, 1, 2, 4).reshape(B, G, NC, W)

    def halves(w1):
        return w1[:W].astype(BF16), w1[W:].astype(BF16)

    k1t, k1b = halves(ck_w1)
    v1t, v1b = halves(cv_w1)
    kc, vc = _compress(chunks(cmp[:, :NSA_KV]), chunks(cmp[:, NSA_KV:]),
                       ck_pos.reshape(2, W), cv_pos.reshape(2, W),
                       k1t, k1b, ck_w2.astype(BF16), v1t, v1b, cv_w2.astype(BF16))
    return _nsa_attention(proj, kc, vc, gates, _importance_matrix_t(NC), B, S)


def kernel(x, positions, ln_gain, ln_bias, diff_w_in, diff_lambda_q1, diff_lambda_k1, diff_lambda_q2, diff_lambda_k2, diff_subln_gain, diff_w_o, nsa_w_in, nsa_cmp_k_pos, nsa_cmp_k_w1, nsa_cmp_k_w2, nsa_cmp_v_pos, nsa_cmp_v_w1, nsa_cmp_v_w2, nsa_w_o, moe_router_w, moe_router_b, moe_w_gate_up, moe_b_gate_up, moe_w_down, moe_b_down):
    B, S, D = x.shape
    T = B * S
    assert D == D_MODEL and S % 1024 == 0 and S // SEL_LEN <= LANES

    inv = ROPE_THETA ** (-jnp.arange(0, HEAD_DIM, 2, dtype=F32) / HEAD_DIM)
    ang = positions.astype(F32).reshape(T, 1) * inv[None, :]
    cos2 = jnp.concatenate([jnp.cos(ang), jnp.cos(ang)], axis=1)
    sin2 = jnp.concatenate([-jnp.sin(ang), jnp.sin(ang)], axis=1)

    w_gu = moe_w_gate_up
    w_dn = moe_w_down
    b_gu = moe_b_gate_up.reshape(DEPTH, N_EXPERTS, 1, 2 * D_FF)
    b_dn = moe_b_down.reshape(DEPTH, N_EXPERTS, 1, D)
    rw = jnp.pad(moe_router_w, ((0, 0), (0, 0), (0, LANES - N_EXPERTS)))
    rw_hi = rw.astype(BF16)
    rw_lo = (rw - rw_hi.astype(F32)).astype(BF16)
    rb = jnp.pad(moe_router_b, ((0, 0), (0, LANES - N_EXPERTS))).reshape(DEPTH, 1, LANES)

    h = x.reshape(T, D)
    h_bf = h.astype(BF16)
    for li in range(DEPTH):
        j = li // 2
        if li % 2 == 0:
            lambda_init = 0.8 - 0.6 * math.exp(-0.3 * li)
            proj = _proj(h_bf, diff_w_in[j].astype(BF16), cos2, sin2,
                         4 * DIFF_HEADS * HEAD_DIM // 1024, BF16, tn=1024)
            lamv = jnp.stack([diff_lambda_q1[j], diff_lambda_k1[j], diff_lambda_q2[j], diff_lambda_k2[j]])
            mix = _diff_attention(proj, lamv, diff_subln_gain[j], lambda_init, B, S)
            w_o = diff_w_o[j].astype(BF16)
        else:
            mix = _nsa_mixer(h_bf, cos2, sin2, nsa_w_in[j], nsa_cmp_k_pos[j], nsa_cmp_k_w1[j], nsa_cmp_k_w2[j],
                             nsa_cmp_v_pos[j], nsa_cmp_v_w1[j], nsa_cmp_v_w2[j], B, S)
            w_o = nsa_w_o[j].astype(BF16)
        h, topi, gate, rank, counts = _out_ln_route(
            mix, w_o, h, ln_gain[li, 0].reshape(1, D), ln_bias[li, 0].reshape(1, D),
            rw_hi[li], rw_lo[li], rb[li])
        h, h_bf = _moe_layer(h, topi, gate, rank, counts, li, w_gu, b_gu, w_dn, b_dn,
                             ln_gain[li, 1].reshape(1, D), ln_bias[li, 1].reshape(1, D))
    return h.reshape(B, S, D)
```
